```python
import functools
import jax, jax.numpy as jnp
from jax import lax
import numpy as np

D_MODEL = 1024
BATCH = 2
SEQ = 8192
DEPTH = 2
DEC_BATCH = 128
DEC_SEQ = 8
PAST_LEN = 2048
PAGE_SIZE = 128

BRANCH_W = D_MODEL // 2
GLA_H = 4
GLA_DK = BRANCH_W // (2 * GLA_H)
GLA_DV = BRANCH_W // GLA_H
GLA_LR = 16
GLA_TAU = 16.0
GLA_CHUNK = 64
FOX_HD = 64
FOX_H = BRANCH_W // FOX_HD
FOX_QBLK = 128
CONV_D = BRANCH_W
CONV_W = 3
N_BRANCH = 3
N_EXPERTS = 16
N_GROUPS = 4
EXP_PER_GROUP = N_EXPERTS // N_GROUPS
TOP_K = 2
D_EXPERT = D_MODEL // 4
LN_EPS = 1e-5
DEEPNORM_ALPHA = (2 * DEPTH) ** 0.25
DEEPNORM_BETA = (8 * DEPTH) ** -0.25

IN_SPLITS = (
    ('gla_q', GLA_H * GLA_DK), ('gla_k', GLA_H * GLA_DK), ('gla_v', GLA_H * GLA_DV),
    ('gla_a', GLA_LR), ('gla_r', GLA_H * GLA_DV),
    ('fox_q', FOX_H * FOX_HD), ('fox_k', FOX_H * FOX_HD), ('fox_v', FOX_H * FOX_HD),
    ('fox_f', FOX_H),
    ('conv_b', CONV_D), ('conv_c', CONV_D), ('conv_h', CONV_D),
    ('gate', N_BRANCH * D_MODEL),
)
D_IN = sum(n for _, n in IN_SPLITS)

kernel_name = 'hybrid_gla_fox_conv_moe_step'


def _split_in(proj):
    out = {}
    off = 0
    for name, n in IN_SPLITS:
        out[name] = proj[..., off:off + n]
        off += n
    return out


def _layernorm(x, g, b):
    xf = x.astype(jnp.float32)
    mu = jnp.mean(xf, -1, keepdims=True)
    var = jnp.mean(jnp.square(xf - mu), -1, keepdims=True)
    return ((xf - mu) * lax.rsqrt(var + LN_EPS) * g + b).astype(x.dtype)


def _gla_chunked(q, k, v, log_a, s0):
    B, T, H, DK = q.shape
    DV = v.shape[-1]
    C = GLA_CHUNK if T % GLA_CHUNK == 0 else T
    N = T // C
    f32 = jnp.float32
    qf = q.astype(f32).reshape(B, N, C, H, DK)
    kf = k.astype(f32).reshape(B, N, C, H, DK)
    vf = v.astype(f32).reshape(B, N, C, H, DV)
    b = jnp.cumsum(log_a.astype(f32).reshape(B, N, C, H, DK), axis=2)
    b_last = b[:, :, -1:]
    q_in = qf * jnp.exp(b)
    k_in = kf * jnp.exp(-b)
    k_end = kf * jnp.exp(b_last - b)
    causal = jnp.tril(jnp.ones((C, C), dtype=bool))
    att = jnp.where(causal, jnp.einsum('bnthk,bnshk->bnhts', q_in, k_in), 0.0)
    o_intra = jnp.einsum('bnhts,bnshv->bnthv', att, vf)
    ds = jnp.einsum('bnshk,bnshv->bnhkv', k_end, vf)
    decay = jnp.exp(b_last[:, :, 0])

    def step(S, inp):
        d, dS = inp
        return d[..., None] * S + dS, S

    s_fin, s_start = lax.scan(step, s0.astype(f32),
                              (jnp.moveaxis(decay, 1, 0), jnp.moveaxis(ds, 1, 0)))
    o_inter = jnp.einsum('bnthk,nbhkv->bnthv', q_in, s_start)
    return (o_intra + o_inter).reshape(B, T, H, DV), s_fin


def _fox_block(q_blk, cq_blk, qpos, k, v, ckT, kpos):
    s = jnp.einsum('bqhd,bkhd->bhqk', q_blk, k, preferred_element_type=jnp.float32) * (FOX_HD ** -0.5)
    s = s + (jnp.swapaxes(cq_blk, 1, 2)[..., :, None] - ckT[..., None, :])
    s = jnp.where(kpos[None, :] <= qpos[:, None], s, -jnp.inf)
    p = jax.nn.softmax(s, axis=-1)
    return jnp.einsum('bhqk,bkhd->bqhd', p.astype(v.dtype), v)


def _fox_prompt(q, k, v, logf):
    B, T, H, _ = q.shape
    c = jnp.cumsum(logf, axis=1)
    ckT = jnp.swapaxes(c, 1, 2)
    nb = T // FOX_QBLK
    pos = jnp.arange(T)
    qb = jnp.swapaxes(q.reshape(B, nb, FOX_QBLK, H, FOX_HD), 0, 1)
    cb = jnp.swapaxes(c.reshape(B, nb, FOX_QBLK, H), 0, 1)
    qposb = pos.reshape(nb, FOX_QBLK)
    o = lax.map(lambda a: _fox_block(a[0], a[1], a[2], k, v, ckT, pos), (qb, cb, qposb))
    return jnp.swapaxes(o, 0, 1).reshape(B, T, H, FOX_HD)


def _fox_sample(q, k, v, logf, k_pool, v_pool, logf_pool, page_table):
    DB, T = q.shape[:2]
    past = page_table.shape[1] * PAGE_SIZE
    k_past = jnp.take(k_pool, page_table, axis=0).reshape(DB, past, FOX_H, FOX_HD)
    v_past = jnp.take(v_pool, page_table, axis=0).reshape(DB, past, FOX_H, FOX_HD)
    lf_past = jnp.take(logf_pool, page_table, axis=0).reshape(DB, past, FOX_H)
    k_all = jnp.concatenate([k_past.astype(k.dtype), k], axis=1)
    v_all = jnp.concatenate([v_past.astype(v.dtype), v], axis=1)
    c = jnp.cumsum(jnp.concatenate([lf_past.astype(jnp.float32), logf], axis=1), axis=1)
    kpos = jnp.arange(past + T)
    qpos = past + jnp.arange(T)
    return _fox_block(q, c[:, past:], qpos, k_all, v_all, jnp.swapaxes(c, 1, 2), kpos)


def _short_conv(u, buf, w):
    T = u.shape[1]
    full = jnp.concatenate([buf.astype(u.dtype), u], axis=1)
    y = w[0] * full[:, 0:T]
    for j in range(1, CONV_W):
        y = y + w[j] * full[:, j:j + T]
    return y, full[:, T:]


def _moe(x, w_router, b_router, w_gate, w_up, w_down):
    B, T, D = x.shape
    t = x.reshape(B * T, D)
    logits = jnp.einsum('nd,de->ne', t, w_router, preferred_element_type=jnp.float32) + b_router.astype(jnp.float32)
    probs = jax.nn.softmax(logits, axis=-1)
    grouped = probs.reshape(-1, N_GROUPS, EXP_PER_GROUP)
    gscore = lax.top_k(grouped, TOP_K)[0].sum(-1)
    gsel = jnp.argmax(gscore, axis=-1)
    in_grp = jnp.arange(N_GROUPS)[None, :] == gsel[:, None]
    masked = jnp.where(in_grp[:, :, None], grouped, -1.0).reshape(-1, N_EXPERTS)
    vals, idx = lax.top_k(masked, TOP_K)
    wts = vals / jnp.sum(vals, -1, keepdims=True)
    gate = jnp.einsum('nke,nk->ne', jax.nn.one_hot(idx, N_EXPERTS, dtype=jnp.float32), wts)
    h = jax.nn.silu(jnp.einsum('nd,edf->nef', t, w_gate)) * jnp.einsum('nd,edf->nef', t, w_up)
    h = h * gate[:, :, None].astype(h.dtype)
    return jnp.einsum('nef,efd->nd', h, w_down).reshape(B, T, D)


def _layer(x, w_in, w_gla_a2, b_gla_a, gla_norm_g, b_fox_f, conv_w, w_branch, w_out,
           ln1_g, ln1_b, w_router, b_router, w_e_gate, w_e_up, w_e_down, ln2_g, ln2_b,
           gla_s0, conv_buf, fox_fn):
    B, T, _ = x.shape
    p = _split_in(jnp.einsum('btd,dn->btn', x, w_in))
    gq = p['gla_q'].reshape(B, T, GLA_H, GLA_DK) * (GLA_DK ** -0.5)
    gk = p['gla_k'].reshape(B, T, GLA_H, GLA_DK)
    gv = p['gla_v'].reshape(B, T, GLA_H, GLA_DV)
    a_pre = jnp.einsum('btr,rk->btk', p['gla_a'], w_gla_a2) + b_gla_a
    log_a = (jax.nn.log_sigmoid(a_pre.astype(jnp.float32)) / GLA_TAU).reshape(B, T, GLA_H, GLA_DK)
    o, gla_state = _gla_chunked(gq, gk, gv, log_a, gla_s0)
    o = o * lax.rsqrt(jnp.mean(o * o, -1, keepdims=True) + LN_EPS) * gla_norm_g.reshape(GLA_H, GLA_DV)
    o_gla = (o.reshape(B, T, -1) * jax.nn.silu(p['gla_r'].astype(jnp.float32))).astype(x.dtype)
    fq = p['fox_q'].reshape(B, T, FOX_H, FOX_HD)
    fk = p['fox_k'].reshape(B, T, FOX_H, FOX_HD)
    fv = p['fox_v'].reshape(B, T, FOX_H, FOX_HD)
    logf = jax.nn.log_sigmoid((p['fox_f'] + b_fox_f).astype(jnp.float32))
    o_fox = fox_fn(fq, fk, fv, logf).reshape(B, T, -1).astype(x.dtype)
    u = p['conv_c'] * p['conv_h']
    y, conv_state = _short_conv(u, conv_buf, conv_w)
    o_conv = (p['conv_b'] * y).astype(x.dtype)
    ob = jnp.stack([o_gla, o_fox, o_conv], axis=2)
    br = jnp.einsum('btjc,jcd->btjd', ob, w_branch)
    g = jax.nn.sigmoid(p['gate'].reshape(B, T, N_BRANCH, D_MODEL))
    mix = jnp.einsum('btd,de->bte', jnp.sum(g * br, axis=2), w_out)
    x = _layernorm(DEEPNORM_ALPHA * x + mix, ln1_g, ln1_b)
    x = _layernorm(DEEPNORM_ALPHA * x + _moe(x, w_router, b_router, w_e_gate, w_e_up, w_e_down), ln2_g, ln2_b)
    return x, (fk, fv, logf, gla_state, conv_state)


def setup_inputs(seed: int = 0) -> dict:
    key = jax.random.key(seed)
    ks = jax.random.split(key, 32)
    f32 = jnp.float32
    n_pages = PAST_LEN // PAGE_SIZE
    n_pool = (DEC_BATCH * n_pages * 5 + 3) // 4

    def nrm(k, shape, scale=1.0):
        return jax.random.normal(k, shape, f32) * scale

    page_table = jax.random.permutation(ks[0], n_pool)[:DEC_BATCH * n_pages].reshape(DEC_BATCH, n_pages).astype(jnp.int32)
    return {
        'x_prompt': nrm(ks[1], (BATCH, SEQ, D_MODEL)),
        'x_sample': nrm(ks[2], (DEC_BATCH, DEC_SEQ, D_MODEL)),
        'cache_fox_k': nrm(ks[3], (DEPTH, n_pool, PAGE_SIZE, FOX_H, FOX_HD)),
        'cache_fox_v': nrm(ks[4], (DEPTH, n_pool, PAGE_SIZE, FOX_H, FOX_HD)),
        'cache_fox_logf': jax.nn.log_sigmoid(4.0 + nrm(ks[5], (DEPTH, n_pool, PAGE_SIZE, FOX_H), 1.5)),
        'state_gla': nrm(ks[6], (DEPTH, DEC_BATCH, GLA_H, GLA_DK, GLA_DV)),
        'state_conv': nrm(ks[7], (DEPTH, DEC_BATCH, CONV_W - 1, CONV_D)),
        'page_table': page_table,
        'w_in': nrm(ks[8], (DEPTH, D_MODEL, D_IN), D_MODEL ** -0.5),
        'w_gla_a2': nrm(ks[9], (DEPTH, GLA_LR, GLA_H * GLA_DK), GLA_LR ** -0.5),
        'b_gla_a': nrm(ks[10], (DEPTH, GLA_H * GLA_DK), 0.1),
        'gla_norm_g': 1.0 + nrm(ks[11], (DEPTH, GLA_H * GLA_DV), 0.02),
        'b_fox_f': jax.random.uniform(ks[12], (DEPTH, FOX_H), f32, 1.0, 7.0),
        'conv_w': nrm(ks[13], (DEPTH, CONV_W, CONV_D), CONV_W ** -0.5),
        'w_branch': nrm(ks[14], (DEPTH, N_BRANCH, BRANCH_W, D_MODEL), DEEPNORM_BETA * BRANCH_W ** -0.5),
        'w_out': nrm(ks[15], (DEPTH, D_MODEL, D_MODEL), DEEPNORM_BETA * D_MODEL ** -0.5),
        'ln1_g': 1.0 + nrm(ks[16], (DEPTH, D_MODEL), 0.02),
        'ln1_b': nrm(ks[17], (DEPTH, D_MODEL), 0.02),
        'w_router': nrm(ks[18], (D_MODEL, N_EXPERTS), D_MODEL ** -0.5),
        'b_router': nrm(ks[19], (N_EXPERTS,), 0.01),
        'w_e_gate': nrm(ks[20], (DEPTH, N_EXPERTS, D_MODEL, D_EXPERT), D_MODEL ** -0.5),
        'w_e_up': nrm(ks[21], (DEPTH, N_EXPERTS, D_MODEL, D_EXPERT), D_MODEL ** -0.5),
        'w_e_down': nrm(ks[22], (DEPTH, N_EXPERTS, D_EXPERT, D_MODEL), DEEPNORM_BETA * D_EXPERT ** -0.5),
        'ln2_g': 1.0 + nrm(ks[23], (DEPTH, D_MODEL), 0.02),
        'ln2_b': nrm(ks[24], (DEPTH, D_MODEL), 0.02),
    }


def reference(x_prompt, x_sample, cache_fox_k, cache_fox_v, cache_fox_logf, state_gla, state_conv,
              page_table, w_in, w_gla_a2, b_gla_a, gla_norm_g, b_fox_f, conv_w, w_branch, w_out,
              ln1_g, ln1_b, w_router, b_router, w_e_gate, w_e_up, w_e_down, ln2_g, ln2_b):
    B = x_prompt.shape[0]
    xp, xs = x_prompt, x_sample
    pk, pv, plf, pgla, pconv = [], [], [], [], []
    sk, sv, slf, sgla, sconv = [], [], [], [], []
    for l in range(DEPTH):
        wl = (w_in[l], w_gla_a2[l], b_gla_a[l], gla_norm_g[l], b_fox_f[l], conv_w[l], w_branch[l],
              w_out[l], ln1_g[l], ln1_b[l], w_router, b_router, w_e_gate[l], w_e_up[l], w_e_down[l],
              ln2_g[l], ln2_b[l])
        xp, st_p = _layer(xp, *wl,
                          jnp.zeros((B, GLA_H, GLA_DK, GLA_DV), jnp.float32),
                          jnp.zeros((B, CONV_W - 1, CONV_D), xp.dtype),
                          _fox_prompt)
        fox_s = functools.partial(_fox_sample, k_pool=cache_fox_k[l], v_pool=cache_fox_v[l],
                                  logf_pool=cache_fox_logf[l], page_table=page_table)
        xs, st_s = _layer(xs, *wl, state_gla[l], state_conv[l], fox_s)
        pk.append(st_p[0]); pv.append(st_p[1]); plf.append(st_p[2]); pgla.append(st_p[3]); pconv.append(st_p[4])
        sk.append(st_s[0]); sv.append(st_s[1]); slf.append(st_s[2]); sgla.append(st_s[3]); sconv.append(st_s[4])
    return (xp, xs,
            jnp.stack(pk), jnp.stack(pv), jnp.stack(plf), jnp.stack(pgla), jnp.stack(pconv),
            jnp.stack(sk), jnp.stack(sv), jnp.stack(slf), jnp.stack(sgla), jnp.stack(sconv))
```

```python
import functools

import jax
import jax.numpy as jnp
from jax import lax
from jax.experimental import pallas as pl
from jax.experimental.pallas import tpu as pltpu

F32 = jnp.float32
BF16 = jnp.bfloat16

D_MODEL = 1024
DEPTH = 2
PAGE_SIZE = 128
BRANCH_W = D_MODEL // 2
GLA_H = 4
GLA_DK = BRANCH_W // (2 * GLA_H)
GLA_DV = BRANCH_W // GLA_H
GLA_LR = 16
GLA_TAU = 16.0
GLA_CHUNK = 64
FOX_HD = 64
FOX_H = BRANCH_W // FOX_HD
CONV_D = BRANCH_W
CONV_W = 3
N_BRANCH = 3
N_EXPERTS = 16
N_GROUPS = 4
EXP_PER_GROUP = N_EXPERTS // N_GROUPS
D_EXPERT = D_MODEL // 4
LN_EPS = 1e-5
DEEPNORM_ALPHA = (2 * DEPTH) ** 0.25

LANES = 128
SUBLANES = 8
VMEM_LIMIT = 56 * 1024 * 1024

_NT = (((1,), (1,)), ((), ()))
_TN = (((0,), (0,)), ((), ()))


def _dot(a, b):
    return jnp.dot(a, b, preferred_element_type=F32)


def _dot_nt(a, b):
    return lax.dot_general(a, b, _NT, preferred_element_type=F32)


def _dot_tn(a, b):
    return lax.dot_general(a, b, _TN, preferred_element_type=F32)


def _split3(x):
    hi = x.astype(BF16)
    r1 = x - hi.astype(F32)
    mid = r1.astype(BF16)
    lo = (r1 - mid.astype(F32)).astype(BF16)
    return hi, mid, lo


def _log_sigmoid(x):
    return jnp.minimum(x, 0.0) - jnp.log1p(jnp.exp(-jnp.abs(x)))


def _sigmoid(x):
    return 1.0 / (1.0 + jnp.exp(-x))


def _layernorm(y, g, b):
    mu = jnp.mean(y, axis=-1, keepdims=True)
    d = y - mu
    var = jnp.mean(d * d, axis=-1, keepdims=True)
    return d * lax.rsqrt(var + LN_EPS) * g + b


def _lane_prefix(x):
    r = lax.broadcasted_iota(jnp.int32, (LANES, 2 * LANES), 0)
    c = lax.broadcasted_iota(jnp.int32, (LANES, 2 * LANES), 1)
    u = jnp.where(r <= c, 1.0, 0.0).astype(BF16)
    hi, mid, lo = _split3(x)
    y = _dot(hi, u) + _dot(mid, u) + _dot(lo, u)
    return y[:, :LANES], y[:, LANES:]


def _const_spec(shape):
    nd = len(shape)
    return pl.BlockSpec(shape, lambda *_: (0,) * nd, pipeline_mode=pl.Buffered(1))


def _params(*sem):
    return pltpu.CompilerParams(dimension_semantics=sem, vmem_limit_bytes=VMEM_LIMIT)


def _row_tile(rows, want):
    t = min(rows, want)
    assert rows % t == 0
    return t


AUG = 2 * FOX_HD
AUG_W = FOX_H * AUG


def _gla_cols(xb, wqkv_ref, wa_ref, wa2_ref, ba_ref, wr_ref, gqkv_ref, gla_ref, gr_ref):
    gqkv_ref[...] = _dot(xb, wqkv_ref[...])
    pa = _dot(xb, wa_ref[...])
    a_pre = _dot(pa.astype(BF16), wa2_ref[...]) + ba_ref[...]
    gla_ref[...] = _log_sigmoid(a_pre) * (1.0 / GLA_TAU)
    r = _dot(xb, wr_ref[...])
    gr_ref[...] = r * _sigmoid(r)


def _proj1_prompt_kernel(x_ref, wqkv_ref, wa_ref, wa2_ref, ba_ref, wr_ref, wfk_ref, wfv_ref, wff_ref, bff_ref,
                         wqa_ref, bqa_ref, wka_ref, wva_ref, bva_ref,
                         gqkv_ref, gla_ref, gr_ref, fk_ref, fv_ref, lf_ref, qa_ref, ka_ref, va_ref,
                         carry, *, tiles_per_seq):
    tm = x_ref.shape[0]

    @pl.when(pl.program_id(0) % tiles_per_seq == 0)
    def _():
        carry[...] = jnp.zeros_like(carry)

    xb = x_ref[...].astype(BF16)
    _gla_cols(xb, wqkv_ref, wa_ref, wa2_ref, ba_ref, wr_ref, gqkv_ref, gla_ref, gr_ref)
    fk_ref[...] = _dot(xb, wfk_ref[...])
    fv_ref[...] = _dot(xb, wfv_ref[...])
    lf = _log_sigmoid(_dot(xb, wff_ref[...]) + bff_ref[...])
    lf_ref[...] = lf[:, :FOX_H]
    ri = lax.broadcasted_iota(jnp.int32, (tm, tm), 0)
    ci = lax.broadcasted_iota(jnp.int32, (tm, tm), 1)
    lmat = jnp.where(ci <= ri, 1.0, 0.0).astype(BF16)
    hi, mid, lo = _split3(lf)
    c = _dot(lmat, hi) + _dot(lmat, mid) + _dot(lmat, lo) + carry[0:1, :]
    carry[...] = jnp.broadcast_to(c[tm - 1:tm, :], carry.shape)
    pr = lax.broadcasted_iota(jnp.int32, (LANES, AUG_W), 0)
    pc = lax.broadcasted_iota(jnp.int32, (LANES, AUG_W), 1)
    ka = _dot(xb, wka_ref[...])
    for j, part in enumerate(_split3(-c)):
        place = jnp.where(pc == pr * AUG + (FOX_HD + j), 1.0, 0.0).astype(BF16)
        ka = ka + _dot(part, place)
    ka_ref[...] = ka.astype(BF16)
    qa_ref[...] = (_dot(xb, wqa_ref[...]) + bqa_ref[...]).astype(BF16)
    va_ref[...] = (_dot(xb, wva_ref[...]) + bva_ref[...]).astype(BF16)


def _proj1_sample_kernel(x_ref, wqkv_ref, wa_ref, wa2_ref, ba_ref, wr_ref, wfk_ref, wfv_ref, wff_ref, bff_ref,
                         wfq_ref, wfft_ref, bfft_ref,
                         gqkv_ref, gla_ref, gr_ref, fk_ref, fv_ref, lf_ref, fq_ref, lft_ref):
    xb = x_ref[...].astype(BF16)
    _gla_cols(xb, wqkv_ref, wa_ref, wa2_ref, ba_ref, wr_ref, gqkv_ref, gla_ref, gr_ref)
    fk_ref[...] = _dot(xb, wfk_ref[...])
    fv_ref[...] = _dot(xb, wfv_ref[...])
    lf_ref[...] = _log_sigmoid(_dot(xb, wff_ref[...]) + bff_ref[...])[:, :FOX_H]
    fq_ref[...] = _dot(xb, wfq_ref[...])
    pft = _dot_nt(wfft_ref[...], xb)[:FOX_H] + bfft_ref[...]
    lft_ref[...] = _log_sigmoid(pft)


def _proj1(x, w, prompt, tg):
    rows = x.shape[0]
    tm = _row_tile(rows, 256)
    row = lambda n: pl.BlockSpec((tm, n), lambda i: (i, 0))
    common = ('wqkv', 'wa', 'wa2', 'ba', 'wr', 'wfk', 'wfv', 'wff', 'bff')
    sds = lambda n, dt: jax.ShapeDtypeStruct((rows, n), dt)
    out_shape = [sds(1024, F32), sds(256, F32), sds(512, F32), sds(512, F32), sds(512, F32), sds(FOX_H, F32)]
    out_specs = [row(1024), row(256), row(512), row(512), row(512), row(FOX_H)]
    if prompt:
        assert tg % tm == 0
        names = common + ('wqa', 'bqa', 'wka', 'wva', 'bva')
        kern = functools.partial(_proj1_prompt_kernel, tiles_per_seq=tg // tm)
        out_shape += [sds(AUG_W, BF16)] * 3
        out_specs += [row(AUG_W)] * 3
        scratch = [pltpu.VMEM((SUBLANES, LANES), F32)]
    else:
        names = common + ('wfq', 'wfft', 'bfft')
        kern = _proj1_sample_kernel
        out_shape += [sds(512, F32), jax.ShapeDtypeStruct((FOX_H, rows), F32)]
        out_specs += [row(512), pl.BlockSpec((FOX_H, tm), lambda i: (0, i))]
        scratch = []
    ws = [w[n] for n in names]
    return pl.pallas_call(
        kern, grid=(rows // tm,),
        in_specs=[row(D_MODEL)] + [_const_spec(a.shape) for a in ws],
        out_specs=out_specs, out_shape=out_shape, scratch_shapes=scratch,
        compiler_params=_params('arbitrary'), name='proj1_prompt' if prompt else 'proj1_sample')(x, *ws)


def _proj2_kernel(x_ref, wc_ref, wg_ref, bch_ref, gate_ref):
    xb = x_ref[...].astype(BF16)
    bch_ref[...] = _dot(xb, wc_ref[...])
    gate_ref[...] = _dot(xb, wg_ref[...])


def _proj2(x, w):
    rows = x.shape[0]
    tm = _row_tile(rows, 256)
    row = lambda n: pl.BlockSpec((tm, n), lambda i: (i, 0))
    return pl.pallas_call(
        _proj2_kernel, grid=(rows // tm,),
        in_specs=[row(D_MODEL), _const_spec(w['wc'].shape), _const_spec(w['wg'].shape)],
        out_specs=(row(3 * CONV_D), row(N_BRANCH * D_MODEL)),
        out_shape=(jax.ShapeDtypeStruct((rows, 3 * CONV_D), F32),
                   jax.ShapeDtypeStruct((rows, N_BRANCH * D_MODEL), F32)),
        compiler_params=_params('parallel'), name='proj2')(x, w['wc'], w['wg'])


def _gla_kernel(qkv_ref, la_ref, r_ref, s0_ref, g_ref, o_ref, sfin_ref, state, *, G, cr, cp, nchunk):
    ti = pl.program_id(1)

    @pl.when(ti == 0)
    def _():
        state[...] = s0_ref[...]

    ri = lax.broadcasted_iota(jnp.int32, (cp, cp), 0)
    ci = lax.broadcasted_iota(jnp.int32, (cp, cp), 1)
    tril = ci <= ri
    lmat = jnp.where(tril, 1.0, 0.0).astype(BF16)
    ones = jnp.ones((cp, GLA_DV), BF16)
    gnorm = g_ref[...]

    def pad(a):
        if cp == cr:
            return a
        return jnp.concatenate([a, jnp.zeros((cp - cr, a.shape[1]), a.dtype)], axis=0)

    for g in range(G):
        def chunk(c, carry, g=g):
            r0 = pl.multiple_of(c * cr, cr)
            qkv = pad(qkv_ref[g, pl.ds(r0, cr), :])
            la = pad(la_ref[g, pl.ds(r0, cr), :])
            q = qkv[:, :GLA_H * GLA_DK]
            k = qkv[:, GLA_H * GLA_DK:2 * GLA_H * GLA_DK]
            vb = qkv[:, 2 * GLA_H * GLA_DK:].astype(BF16)
            la3 = _split3(la)
            b = _dot(lmat, la3[0]) + _dot(lmat, la3[1]) + _dot(lmat, la3[2])
            b_last = b[cp - 1:cp, :]
            q_in = (q * jnp.exp(b)).astype(BF16)
            k_in = (k * jnp.exp(-b)).astype(BF16)
            k_end = (k * jnp.exp(b_last - b)).astype(BF16)
            outs = []
            for h in range(GLA_H):
                ks = slice(h * GLA_DK, (h + 1) * GLA_DK)
                vs = slice(h * GLA_DV, (h + 1) * GLA_DV)
                s_prev = state[g, h]
                att = jnp.where(tril, _dot_nt(q_in[:, ks], k_in[:, ks]), 0.0)
                o = _dot(att.astype(BF16), vb[:, vs]) + _dot(q_in[:, ks], s_prev.astype(BF16))
                dlog = (_dot_tn(la3[0][:, ks], ones) + _dot_tn(la3[1][:, ks], ones)
                        + _dot_tn(la3[2][:, ks], ones))
                state[g, h] = jnp.exp(dlog) * s_prev + _dot_tn(k_end[:, ks], vb[:, vs])
                ms = jnp.mean(o * o, axis=-1, keepdims=True)
                outs.append(o * lax.rsqrt(ms + LN_EPS) * gnorm[:, vs])
            o_all = jnp.concatenate(outs, axis=1)[:cr] * r_ref[g, pl.ds(r0, cr), :]
            o_ref[g, pl.ds(r0, cr), :] = o_all.astype(o_ref.dtype)
            return carry

        lax.fori_loop(0, nchunk, chunk, 0)

    @pl.when(ti == pl.num_programs(1) - 1)
    def _():
        sfin_ref[...] = state[...]


def _gla(gqkv, gla, gr, s0, gnorm, bg, tg, out_dtype):
    if tg % GLA_CHUNK == 0:
        G, cr, cp, tt = 1, GLA_CHUNK, GLA_CHUNK, min(tg, 512)
    else:
        assert tg % SUBLANES == 0 and tg <= 2 * SUBLANES
        G, cr, cp, tt = SUBLANES, tg, 2 * SUBLANES, tg
    assert bg % G == 0 and tg % tt == 0
    blk = lambda n: pl.BlockSpec((G, tt, n), lambda b, t: (b, t, 0))
    sblk = pl.BlockSpec((G, GLA_H, GLA_DK, GLA_DV), lambda b, t: (b, 0, 0, 0))
    kern = functools.partial(_gla_kernel, G=G, cr=cr, cp=cp, nchunk=tt // cr)
    o, sfin = pl.pallas_call(
        kern, grid=(bg // G, tg // tt),
        in_specs=[blk(1024), blk(256), blk(512), sblk, _const_spec(gnorm.shape)],
        out_specs=(blk(512), sblk),
        out_shape=(jax.ShapeDtypeStruct((bg, tg, 512), out_dtype),
                   jax.ShapeDtypeStruct((bg, GLA_H, GLA_DK, GLA_DV), F32)),
        scratch_shapes=[pltpu.VMEM((G, GLA_H, GLA_DK, GLA_DV), F32)],
        compiler_params=_params('parallel', 'arbitrary'), name='gla')(
            gqkv.reshape(bg, tg, 1024), gla.reshape(bg, tg, 256), gr.reshape(bg, tg, 512), s0, gnorm)
    return o.reshape(bg * tg, 512), sfin


def _conv_kernel(bch_ref, buf_ref, w_ref, o_ref, st_ref, prev, *, G, tt):
    @pl.when(pl.program_id(1) == 0)
    def _():
        prev[...] = buf_ref[...]

    w = w_ref[...]
    rowi = lax.broadcasted_iota(jnp.int32, (tt, CONV_D), 0)
    for g in range(G):
        bch = bch_ref[g]
        u = bch[:, CONV_D:2 * CONV_D] * bch[:, 2 * CONV_D:]
        p = prev[g]
        u1 = jnp.where(rowi == 0, p[1:2], pltpu.roll(u, 1, 0))
        u2 = jnp.where(rowi == 0, p[0:1], jnp.where(rowi == 1, p[1:2], pltpu.roll(u, 2, 0)))
        y = w[0:1] * u2 + w[1:2] * u1 + w[2:3] * u
        o_ref[g] = (bch[:, :CONV_D] * y).astype(o_ref.dtype)
        tail = u[tt - (CONV_W - 1):tt]
        prev[g] = tail
        st_ref[g] = tail


def _conv(bch, buf, conv_w, bg, tg, out_dtype):
    if tg >= 512:
        G, tt = 1, 512
    else:
        G, tt = min(bg, 16), tg
    assert bg % G == 0 and tg % tt == 0 and tt >= CONV_W - 1
    sblk = pl.BlockSpec((G, CONV_W - 1, CONV_D), lambda b, t: (b, 0, 0))
    o, st = pl.pallas_call(
        functools.partial(_conv_kernel, G=G, tt=tt), grid=(bg // G, tg // tt),
        in_specs=[pl.BlockSpec((G, tt, 3 * CONV_D), lambda b, t: (b, t, 0)), sblk,
                  _const_spec(conv_w.shape)],
        out_specs=(pl.BlockSpec((G, tt, CONV_D), lambda b, t: (b, t, 0)), sblk),
        out_shape=(jax.ShapeDtypeStruct((bg, tg, CONV_D), out_dtype),
                   jax.ShapeDtypeStruct((bg, CONV_W - 1, CONV_D), F32)),
        scratch_shapes=[pltpu.VMEM((G, CONV_W - 1, CONV_D), F32)],
        compiler_params=_params('parallel', 'arbitrary'), name='conv')(
            bch.reshape(bg, tg, 3 * CONV_D), buf, conv_w)
    return o.reshape(bg * tg, CONV_D), st


def _fox_prompt_kernel(q_ref, k_ref, v_ref, o_ref, m_scr, acc_scr, *, tq, tk, hb):
    qi = pl.program_id(2)
    nfull = (qi * tq) // tk
    row = lax.broadcasted_iota(jnp.int32, (tq, tk), 0)
    col = lax.broadcasted_iota(jnp.int32, (tq, tk), 1)
    m_scr[...] = jnp.full_like(m_scr, -jnp.inf)
    acc_scr[...] = jnp.zeros_like(acc_scr)

    def update(k0, masked):
        for h in range(hb):
            ls = slice(h * AUG, (h + 1) * AUG)
            s = _dot_nt(q_ref[:, ls], k_ref[pl.ds(k0, tk), ls])
            if masked:
                s = jnp.where(col + k0 <= row + qi * tq, s, -jnp.inf)
            m_prev = m_scr[h]
            m_new = jnp.maximum(m_prev, jnp.max(s, axis=-1, keepdims=True))
            alpha = jnp.exp(m_prev - m_new)
            p = jnp.exp(s - jnp.concatenate([m_new] * (tk // LANES), axis=1))
            acc_scr[h] = alpha * acc_scr[h] + _dot(p.astype(BF16), v_ref[pl.ds(k0, tk), ls])
            m_scr[h] = m_new

    def body(kj, carry):
        update(pl.multiple_of(kj * tk, tk), False)
        return carry

    lax.fori_loop(0, nfull, body, 0)
    update(pl.multiple_of(nfull * tk, tk), True)
    lane = lax.broadcasted_iota(jnp.int32, (1, AUG), 1)
    outs = []
    for h in range(0, hb, 2):
        o = []
        for hh in (h, h + 1):
            acc = acc_scr[hh]
            o.append(acc / acc[:, FOX_HD:FOX_HD + 1])
        outs.append(jnp.where(lane < FOX_HD, o[0], pltpu.roll(o[1], FOX_HD, 1)))
    o_ref[...] = jnp.concatenate(outs, axis=1).astype(o_ref.dtype)


def _fox_prompt(qa, ka, va, bg, tg):
    tq = min(tg, 256)
    tk = min(tg, 512)
    hb = 2
    nq = tg // tq
    return pl.pallas_call(
        functools.partial(_fox_prompt_kernel, tq=tq, tk=tk, hb=hb), grid=(bg, FOX_H // hb, nq),
        in_specs=[pl.BlockSpec((tq, hb * AUG), lambda b, p, i: (b * nq + i, p)),
                  pl.BlockSpec((tg, hb * AUG), lambda b, p, i: (b, p)),
                  pl.BlockSpec((tg, hb * AUG), lambda b, p, i: (b, p))],
        out_specs=pl.BlockSpec((tq, hb * FOX_HD), lambda b, p, i: (b * nq + i, p)),
        out_shape=jax.ShapeDtypeStruct((bg * tg, BRANCH_W), BF16),
        scratch_shapes=[pltpu.VMEM((hb, tq, AUG), F32)] * 2,
        compiler_params=_params('parallel', 'parallel', 'arbitrary'), name='fox_prompt')(qa, ka, va)


def _fox_bias_kernel(pt_ref, lfn_ref, *refs, n_pages):
    del pt_ref
    lf_refs, c_ref = refs[:n_pages], refs[n_pages]
    cur = jnp.zeros((FOX_H, LANES), F32)
    for i in range(n_pages):
        pre, tot = _lane_prefix(lf_refs[i][...])
        c_ref[:, i * LANES:(i + 1) * LANES] = cur + pre
        cur = cur + tot
    pre, _ = _lane_prefix(lfn_ref[0])
    c_ref[:, n_pages * LANES:] = cur + pre


def _fox_sample_kernel(pt_ref, q_ref, kn_ref, vn_ref, c_ref, *refs, P, tq, n_pages):
    del pt_ref
    k_refs, v_refs = refs[:P], refs[P:2 * P]
    o_ref, m_scr, l_scr, acc_scr = refs[2 * P:]
    j = pl.program_id(1)
    nrow = tq * FOX_H
    npk = PAGE_SIZE * FOX_H

    @pl.when(j == 0)
    def _():
        m_scr[...] = jnp.full_like(m_scr, -jnp.inf)
        l_scr[...] = jnp.zeros_like(l_scr)
        acc_scr[...] = jnp.zeros_like(acc_scr)

    qm = q_ref[...].astype(BF16)

    def block(kf, vf, c_row, causal):
        n = kf.shape[0]
        row = lax.broadcasted_iota(jnp.int32, (nrow, n), 0)
        col = lax.broadcasted_iota(jnp.int32, (nrow, n), 1)
        keep = (col % FOX_H) == (row % FOX_H)
        if causal:
            keep = keep & ((col // FOX_H) <= (row // FOX_H))
        s = jnp.where(keep, _dot_nt(qm, kf.astype(BF16)) - c_row, -jnp.inf)
        m_prev = m_scr[...]
        m_new = jnp.maximum(m_prev, jnp.max(s, axis=-1, keepdims=True))
        alpha = jnp.exp(m_prev - m_new)
        if n % LANES == 0:
            p = jnp.exp(s - jnp.concatenate([m_new] * (n // LANES), axis=1))
        else:
            p = jnp.exp(s - m_new[:, :n])
        l_scr[...] = alpha * l_scr[...] + jnp.sum(p, axis=-1, keepdims=True)
        acc_scr[...] = alpha[:, :FOX_HD] * acc_scr[...] + _dot(p.astype(BF16), vf.astype(BF16))
        m_scr[...] = m_new

    for i in range(P):
        block(k_refs[i][...].reshape(npk, FOX_HD), v_refs[i][...].reshape(npk, FOX_HD),
              c_ref[pl.ds(j * P + i, 1), :], False)

    @pl.when(j == pl.num_programs(1) - 1)
    def _():
        block(kn_ref[...], vn_ref[...], c_ref[n_pages:n_pages + 1, :nrow], True)
        o_ref[...] = acc_scr[...] / l_scr[:, :FOX_HD]


def _fox_sample(fq, fk32, fv32, lft, layer, k_cache, v_cache, lf_cache, page_table, db, tq):
    n_pages = page_table.shape[1]
    P = 4 if n_pages % 4 == 0 else 1
    nrow = tq * FOX_H
    npk = PAGE_SIZE * FOX_H
    assert nrow <= LANES
    lfp = jnp.swapaxes(lf_cache[layer], 1, 2)
    lfn = jnp.swapaxes(lft.reshape(FOX_H, db, tq), 0, 1)
    lfn = jnp.pad(lfn, ((0, 0), (0, 0), (0, LANES - tq)))
    width = (n_pages + 1) * LANES
    c = pl.pallas_call(
        functools.partial(_fox_bias_kernel, n_pages=n_pages),
        grid_spec=pltpu.PrefetchScalarGridSpec(
            num_scalar_prefetch=1, grid=(db,),
            in_specs=[pl.BlockSpec((1, FOX_H, LANES), lambda b, pt: (b, 0, 0))]
                     + [pl.BlockSpec((None, FOX_H, PAGE_SIZE), lambda b, pt, i=i: (pt[b, i], 0, 0))
                        for i in range(n_pages)],
            out_specs=pl.BlockSpec((None, FOX_H, width), lambda b, pt: (b, 0, 0))),
        out_shape=jax.ShapeDtypeStruct((db, FOX_H, width), F32),
        compiler_params=_params('parallel'), name='fox_sample_bias')(page_table, lfn, *([lfp] * n_pages))
    c = jnp.swapaxes(c, 1, 2).reshape(db, n_pages + 1, npk)
    seq = lambda: pl.BlockSpec((None, nrow, FOX_HD), lambda b, j, pt: (b, 0, 0))
    page = lambda i: pl.BlockSpec((None, None, PAGE_SIZE, FOX_H, FOX_HD),
                                  lambda b, j, pt, i=i: (layer, pt[b, j * P + i], 0, 0, 0))
    o = pl.pallas_call(
        functools.partial(_fox_sample_kernel, P=P, tq=tq, n_pages=n_pages),
        grid_spec=pltpu.PrefetchScalarGridSpec(
            num_scalar_prefetch=1, grid=(db, n_pages // P),
            in_specs=[seq(), seq(), seq(), pl.BlockSpec((None, n_pages + 1, npk), lambda b, j, pt: (b, 0, 0))]
                     + [page(i) for i in range(P)] + [page(i) for i in range(P)],
            out_specs=seq(),
            scratch_shapes=[pltpu.VMEM((nrow, LANES), F32), pltpu.VMEM((nrow, LANES), F32),
                            pltpu.VMEM((nrow, FOX_HD), F32)]),
        out_shape=jax.ShapeDtypeStruct((db, nrow, FOX_HD), F32),
        compiler_params=_params('parallel', 'arbitrary'), name='fox_sample')(
            page_table, fq.reshape(db, nrow, FOX_HD), fk32.reshape(db, nrow, FOX_HD),
            fv32.reshape(db, nrow, FOX_HD), c, *([k_cache] * P), *([v_cache] * P))
    return o.reshape(db * tq, BRANCH_W)


def _merge_kernel(x_ref, og_ref, of_ref, oc_ref, gate_ref, wb_ref, wo_ref, g_ref, b_ref, y_ref):
    m = None
    for jb, o_ref in enumerate((og_ref, of_ref, oc_ref)):
        gj = _sigmoid(gate_ref[:, jb * D_MODEL:(jb + 1) * D_MODEL])
        term = gj * _dot(o_ref[...].astype(BF16), wb_ref[jb])
        m = term if m is None else m + term
    y = DEEPNORM_ALPHA * x_ref[...] + _dot(m.astype(BF16), wo_ref[...])
    y_ref[...] = _layernorm(y, g_ref[...], b_ref[...])


def _merge(x, o_gla, o_fox, o_conv, gate, w):
    rows = x.shape[0]
    tm = _row_tile(rows, 256)
    row = lambda n: pl.BlockSpec((tm, n), lambda i: (i, 0))
    consts = [w['wb'], w['wo'], w['ln1_g'], w['ln1_b']]
    return pl.pallas_call(
        _merge_kernel, grid=(rows // tm,),
        in_specs=[row(D_MODEL), row(BRANCH_W), row(BRANCH_W), row(BRANCH_W), row(N_BRANCH * D_MODEL)]
                 + [_const_spec(a.shape) for a in consts],
        out_specs=row(D_MODEL), out_shape=jax.ShapeDtypeStruct((rows, D_MODEL), F32),
        compiler_params=_params('parallel'), name='merge')(x, o_gla, o_fox, o_conv, gate, *consts)


def _route(probs):
    rows = [probs[i:i + 1, :] for i in range(N_EXPERTS)]
    gscore = []
    for g in range(N_GROUPS):
        r0, r1, r2, r3 = rows[EXP_PER_GROUP * g:EXP_PER_GROUP * (g + 1)]
        a, b = jnp.maximum(r0, r1), jnp.minimum(r0, r1)
        c, d = jnp.maximum(r2, r3), jnp.minimum(r2, r3)
        gscore.append(jnp.maximum(a, c) + jnp.maximum(jnp.minimum(a, c), jnp.maximum(b, d)))
    best, gsel = gscore[0], jnp.zeros_like(gscore[0], dtype=jnp.int32)
    for g in range(1, N_GROUPS):
        upd = gscore[g] > best
        best = jnp.where(upd, gscore[g], best)
        gsel = jnp.where(upd, g, gsel)
    v = []
    for i in range(EXP_PER_GROUP):
        vi = rows[(N_GROUPS - 1) * EXP_PER_GROUP + i]
        for g in range(N_GROUPS - 2, -1, -1):
            vi = jnp.where(gsel == g, rows[g * EXP_PER_GROUP + i], vi)
        v.append(vi)
    v1, i1 = v[0], jnp.zeros_like(gsel)
    for i in range(1, EXP_PER_GROUP):
        upd = v[i] > v1
        v1 = jnp.where(upd, v[i], v1)
        i1 = jnp.where(upd, i, i1)
    v2, i2 = jnp.full_like(v1, -1.0), jnp.full_like(gsel, -1)
    for i in range(EXP_PER_GROUP):
        cand = jnp.where(i1 == i, -1.0, v[i])
        upd = cand > v2
        v2 = jnp.where(upd, cand, v2)
        i2 = jnp.where(upd, i, i2)
    den = v1 + v2
    w1, w2 = v1 / den, v2 / den
    out = []
    for g in range(N_GROUPS):
        for i in range(EXP_PER_GROUP):
            wi = jnp.where(i1 == i, w1, jnp.where(i2 == i, w2, 0.0))
            out.append(jnp.where(gsel == g, wi, 0.0))
    return out


def _moe_kernel(x_ref, wrt_ref, br_ref, wg_ref, wu_ref, wd_ref, g_ref, b_ref, y_ref):
    x = x_ref[...]
    tm = x.shape[0]
    xb = x.astype(BF16)
    x_lo = (x - xb.astype(F32)).astype(BF16)
    wrt = wrt_ref[...]
    w_hi = wrt.astype(BF16)
    w_lo = (wrt - w_hi.astype(F32)).astype(BF16)
    logits = _dot_nt(w_hi, xb) + _dot_nt(w_hi, x_lo) + _dot_nt(w_lo, xb) + br_ref[...]
    e = jnp.exp(logits - jnp.max(logits, axis=0, keepdims=True))
    probs = e / jnp.sum(e, axis=0, keepdims=True)
    gate_rows = _route(probs)
    gate_t = jnp.concatenate(gate_rows + [jnp.zeros((LANES - N_EXPERTS, tm), F32)], axis=0)
    gate = gate_t.T
    acc = jnp.zeros((tm, D_MODEL), F32)
    for ex in range(N_EXPERTS):
        hg = _dot(xb, wg_ref[ex])
        hu = _dot(xb, wu_ref[ex])
        h = (hg * _sigmoid(hg)) * hu * gate[:, ex:ex + 1]
        acc = acc + _dot(h.astype(BF16), wd_ref[ex])
    y = DEEPNORM_ALPHA * x + acc
    y_ref[...] = _layernorm(y, g_ref[...], b_ref[...])


def _moe(x, w):
    rows = x.shape[0]
    tm = _row_tile(rows, 512)
    row = pl.BlockSpec((tm, D_MODEL), lambda i: (i, 0))
    consts = [w['wrt'], w['br'], w['weg'], w['weu'], w['wed'], w['ln2_g'], w['ln2_b']]
    return pl.pallas_call(
        _moe_kernel, grid=(rows // tm,),
        in_specs=[row] + [_const_spec(a.shape) for a in consts],
        out_specs=row, out_shape=jax.ShapeDtypeStruct((rows, D_MODEL), F32),
        compiler_params=_params('parallel'), name='moe')(x, *consts)


def _prep_weights(l, w_in, w_gla_a2, b_gla_a, gla_norm_g, b_fox_f, conv_w, w_branch, w_out, ln1_g, ln1_b,
                  w_router, b_router, w_e_gate, w_e_up, w_e_down, ln2_g, ln2_b):
    wi = w_in[l]
    off = {}
    o = 0
    for name, n in (('gla_q', 256), ('gla_k', 256), ('gla_v', 512), ('gla_a', GLA_LR), ('gla_r', 512),
                    ('fox_q', 512), ('fox_k', 512), ('fox_v', 512), ('fox_f', FOX_H),
                    ('conv', 3 * CONV_D), ('gate', N_BRANCH * D_MODEL)):
        off[name] = (o, o + n)
        o += n
    col = lambda name: wi[:, off[name][0]:off[name][1]]
    padc = lambda a, n: jnp.pad(a, ((0, 0), (0, n - a.shape[1])))
    row2 = lambda a: a.reshape(1, -1)
    ff = col('fox_f')
    aug = lambda a: jnp.pad(a.reshape(D_MODEL, FOX_H, FOX_HD), ((0, 0), (0, 0), (0, AUG - FOX_HD))).reshape(D_MODEL, AUG_W)
    aug_bias = lambda n: jnp.tile(jnp.pad(jnp.ones((n,), F32), (FOX_HD, AUG - FOX_HD - n)), FOX_H).reshape(1, AUG_W)
    return {
        'wqkv': jnp.concatenate([col('gla_q') * (GLA_DK ** -0.5), col('gla_k'), col('gla_v')], axis=1).astype(BF16),
        'wa': padc(col('gla_a'), LANES).astype(BF16),
        'wa2': jnp.pad(w_gla_a2[l], ((0, LANES - GLA_LR), (0, 0))).astype(BF16),
        'ba': row2(b_gla_a[l]),
        'wr': col('gla_r').astype(BF16),
        'wfq': (col('fox_q') * (FOX_HD ** -0.5)).astype(BF16),
        'wfk': col('fox_k').astype(BF16),
        'wfv': col('fox_v').astype(BF16),
        'wff': padc(ff, LANES).astype(BF16),
        'bff': padc(row2(b_fox_f[l]), LANES),
        'wfft': jnp.pad(ff.T, ((0, 2 * SUBLANES - FOX_H), (0, 0))).astype(BF16),
        'bfft': b_fox_f[l].reshape(FOX_H, 1),
        'wqa': aug(col('fox_q') * (FOX_HD ** -0.5)).astype(BF16),
        'bqa': aug_bias(3),
        'wka': aug(col('fox_k')).astype(BF16),
        'wva': aug(col('fox_v')).astype(BF16),
        'bva': aug_bias(1),
        'wc': col('conv').astype(BF16),
        'wg': col('gate').astype(BF16),
        'gnorm': row2(gla_norm_g[l]),
        'conv_w': conv_w[l],
        'wb': w_branch[l].astype(BF16),
        'wo': w_out[l].astype(BF16),
        'ln1_g': row2(ln1_g[l]), 'ln1_b': row2(ln1_b[l]),
        'wrt': w_router.T, 'br': b_router.reshape(N_EXPERTS, 1),
        'weg': w_e_gate[l].astype(BF16), 'weu': w_e_up[l].astype(BF16), 'wed': w_e_down[l].astype(BF16),
        'ln2_g': row2(ln2_g[l]), 'ln2_b': row2(ln2_b[l]),
    }


def _layer(x, w, bg, tg, gla_s0, conv_buf, pools, act_dtype):
    prompt = pools is None
    gqkv, gla, gr, fk32, fv32, lf, *fox_in = _proj1(x, w, prompt, tg)
    bch, gate = _proj2(x, w)
    o_gla, gla_state = _gla(gqkv, gla, gr, gla_s0, w['gnorm'], bg, tg, act_dtype)
    if prompt:
        o_fox = _fox_prompt(*fox_in, bg, tg)
    else:
        o_fox = _fox_sample(fox_in[0], fk32, fv32, fox_in[1], *pools, bg, tg)
    o_conv, conv_state = _conv(bch, conv_buf, w['conv_w'], bg, tg, act_dtype)
    x = _merge(x, o_gla, o_fox, o_conv, gate, w)
    x = _moe(x, w)
    return x, (fk32.reshape(bg, tg, FOX_H, FOX_HD), fv32.reshape(bg, tg, FOX_H, FOX_HD),
               lf.reshape(bg, tg, FOX_H), gla_state, conv_state)


def kernel(x_prompt, x_sample, cache_fox_k, cache_fox_v, cache_fox_logf, state_gla, state_conv, page_table, w_in, w_gla_a2, b_gla_a, gla_norm_g, b_fox_f, conv_w, w_branch, w_out, ln1_g, ln1_b, w_router, b_router, w_e_gate, w_e_up, w_e_down, ln2_g, ln2_b):
    B, T, _ = x_prompt.shape
    DB, TS, _ = x_sample.shape
    xp = x_prompt.reshape(B * T, D_MODEL)
    xs = x_sample.reshape(DB * TS, D_MODEL)
    depth = w_in.shape[0]
    st_p, st_s = [], []
    for l in range(depth):
        w = _prep_weights(l, w_in, w_gla_a2, b_gla_a, gla_norm_g, b_fox_f, conv_w, w_branch, w_out,
                          ln1_g, ln1_b, w_router, b_router, w_e_gate, w_e_up, w_e_down, ln2_g, ln2_b)
        xp, sp = _layer(xp, w, B, T, jnp.zeros((B, GLA_H, GLA_DK, GLA_DV), F32),
                        jnp.zeros((B, CONV_W - 1, CONV_D), F32), None, BF16)
        xs, ss = _layer(xs, w, DB, TS, state_gla[l], state_conv[l],
                        (l, cache_fox_k, cache_fox_v, cache_fox_logf, page_table), F32)
        st_p.append(sp)
        st_s.append(ss)
    stack = lambda sts, i: jnp.stack([s[i] for s in sts])
    return (xp.reshape(B, T, D_MODEL), xs.reshape(DB, TS, D_MODEL),
            stack(st_p, 0), stack(st_p, 1), stack(st_p, 2), stack(st_p, 3), stack(st_p, 4),
            stack(st_s, 0), stack(st_s, 1), stack(st_s, 2), stack(st_s, 3), stack(st_s, 4))
```

```python
import functools

import jax
import jax.numpy as jnp
from jax import lax
from jax.experimental import pallas as pl
from jax.experimental.pallas import tpu as pltpu

F32 = jnp.float32
BF16 = jnp.bfloat16

D_MODEL = 1024
DEPTH = 2
PAGE_SIZE = 128
BRANCH_W = D_MODEL // 2
GLA_H = 4
GLA_DK = BRANCH_W // (2 * GLA_H)
GLA_DV = BRANCH_W // GLA_H
GLA_LR = 16
GLA_TAU = 16.0
GLA_CHUNK = 64
FOX_HD = 64
FOX_H = BRANCH_W // FOX_HD
CONV_D = BRANCH_W
CONV_W = 3
N_BRANCH = 3
N_EXPERTS = 16
N_GROUPS = 4
EXP_PER_GROUP = N_EXPERTS // N_GROUPS
D_EXPERT = D_MODEL // 4
LN_EPS = 1e-5
DEEPNORM_ALPHA = (2 * DEPTH) ** 0.25

LANES = 128
SUBLANES = 8
VMEM_LIMIT = 56 * 1024 * 1024

_NT = (((1,), (1,)), ((), ()))
_TN = (((0,), (0,)), ((), ()))


def _dot(a, b):
    return jnp.dot(a, b, preferred_element_type=F32)


def _dot_nt(a, b):
    return lax.dot_general(a, b, _NT, preferred_element_type=F32)


def _dot_tn(a, b):
    return lax.dot_general(a, b, _TN, preferred_element_type=F32)


def _split3(x):
    hi = x.astype(BF16)
    r1 = x - hi.astype(F32)
    mid = r1.astype(BF16)
    lo = (r1 - mid.astype(F32)).astype(BF16)
    return hi, mid, lo


def _pieces(x, precise):
    hi = x.astype(BF16)
    if not precise:
        return (hi,)
    return hi, (x - hi.astype(F32)).astype(BF16)


def _mm(xs, w_ref):
    y = _dot(xs[0], w_ref[0])
    if len(xs) == 2:
        y = y + _dot(xs[1], w_ref[0]) + _dot(xs[0], w_ref[1])
    return y


def _mm_nt(w_ref, xs):
    y = _dot_nt(w_ref[0], xs[0])
    if len(xs) == 2:
        y = y + _dot_nt(w_ref[0], xs[1]) + _dot_nt(w_ref[1], xs[0])
    return y


def _pdot(dot, a, b, precise):
    ap, bp = _pieces(a, precise), _pieces(b, precise)
    y = dot(ap[0], bp[0])
    if precise:
        y = y + dot(ap[1], bp[0]) + dot(ap[0], bp[1])
    return y


def _log_sigmoid(x):
    return jnp.minimum(x, 0.0) - jnp.log1p(jnp.exp(-jnp.abs(x)))


def _sigmoid(x):
    return 1.0 / (1.0 + jnp.exp(-x))


def _layernorm(y, g, b):
    mu = jnp.mean(y, axis=-1, keepdims=True)
    d = y - mu
    var = jnp.mean(d * d, axis=-1, keepdims=True)
    return d * lax.rsqrt(var + LN_EPS) * g + b


def _lane_prefix(x):
    r = lax.broadcasted_iota(jnp.int32, (LANES, 2 * LANES), 0)
    c = lax.broadcasted_iota(jnp.int32, (LANES, 2 * LANES), 1)
    u = jnp.where(r <= c, 1.0, 0.0).astype(BF16)
    hi, mid, lo = _split3(x)
    y = _dot(hi, u) + _dot(mid, u) + _dot(lo, u)
    return y[:, :LANES], y[:, LANES:]


def _const_spec(shape):
    nd = len(shape)
    return pl.BlockSpec(shape, lambda *_: (0,) * nd, pipeline_mode=pl.Buffered(1))


def _params(*sem):
    return pltpu.CompilerParams(dimension_semantics=sem, vmem_limit_bytes=VMEM_LIMIT)


def _row_tile(rows, want):
    t = min(rows, want)
    assert rows % t == 0
    return t


AUG = 2 * FOX_HD
AUG_W = FOX_H * AUG


def _gla_cols(xs, wqkv_ref, wa_ref, wa2_ref, ba_ref, wr_ref, gqkv_ref, gla_ref, gr_ref):
    gqkv_ref[...] = _mm(xs, wqkv_ref)
    a_pre = _mm(_pieces(_mm(xs, wa_ref), len(xs) == 2), wa2_ref) + ba_ref[...]
    gla_ref[...] = _log_sigmoid(a_pre) * (1.0 / GLA_TAU)
    r = _mm(xs, wr_ref)
    gr_ref[...] = r * _sigmoid(r)


def _proj1_prompt_kernel(x_ref, wqkv_ref, wa_ref, wa2_ref, ba_ref, wr_ref, wfk_ref, wfv_ref, wff_ref, bff_ref,
                         wfft_ref, bfft_ref, wqa_ref, bqa_ref, wka_ref, wva_ref, bva_ref,
                         gqkv_ref, gla_ref, gr_ref, fk_ref, fv_ref, lf_ref, qa_ref, ka_ref, va_ref,
                         *rest, tiles_per_seq, precise):
    tm = x_ref.shape[0]
    lo_refs, carry = rest[:-1], rest[-1]

    @pl.when(pl.program_id(0) % tiles_per_seq == 0)
    def _():
        carry[...] = jnp.zeros_like(carry)

    xs = _pieces(x_ref[...], precise)
    _gla_cols(xs, wqkv_ref, wa_ref, wa2_ref, ba_ref, wr_ref, gqkv_ref, gla_ref, gr_ref)
    fk_ref[...] = _mm_nt(wfk_ref, xs)
    fv_ref[...] = _mm_nt(wfv_ref, xs)
    lf_ref[...] = _log_sigmoid(_mm_nt(wfft_ref, xs)[:FOX_H] + bfft_ref[...])
    lf = _log_sigmoid(_mm(xs, wff_ref) + bff_ref[...])
    ri = lax.broadcasted_iota(jnp.int32, (tm, tm), 0)
    ci = lax.broadcasted_iota(jnp.int32, (tm, tm), 1)
    lmat = jnp.where(ci <= ri, 1.0, 0.0).astype(BF16)
    hi, mid, lo = _split3(lf)
    c = _dot(lmat, hi) + _dot(lmat, mid) + _dot(lmat, lo) + carry[0:1, :]
    carry[...] = jnp.broadcast_to(c[tm - 1:tm, :], carry.shape)
    pr = lax.broadcasted_iota(jnp.int32, (LANES, AUG_W), 0)
    pc = lax.broadcasted_iota(jnp.int32, (LANES, AUG_W), 1)
    ka = _mm(xs, wka_ref)
    for j, part in enumerate(_split3(-c)):
        place = jnp.where(pc == pr * AUG + (FOX_HD + j), 1.0, 0.0).astype(BF16)
        ka = ka + _dot(part, place)
    qa = _mm(xs, wqa_ref) + bqa_ref[...]
    va = _mm(xs, wva_ref) + bva_ref[...]
    for i, (val, hi_ref) in enumerate(((qa, qa_ref), (ka, ka_ref), (va, va_ref))):
        hi = val.astype(BF16)
        hi_ref[...] = hi
        if precise:
            lo_refs[i][...] = (val - hi.astype(F32)).astype(BF16)


def _proj1_sample_kernel(x_ref, wqkv_ref, wa_ref, wa2_ref, ba_ref, wr_ref, wfk_ref, wfv_ref, wff_ref, bff_ref,
                         wfft_ref, bfft_ref, wfq_ref,
                         gqkv_ref, gla_ref, gr_ref, fk_ref, fv_ref, lf_ref, fq_ref, lft_ref):
    xs = _pieces(x_ref[...], False)
    _gla_cols(xs, wqkv_ref, wa_ref, wa2_ref, ba_ref, wr_ref, gqkv_ref, gla_ref, gr_ref)
    fk_ref[...] = _mm(xs, wfk_ref)
    fv_ref[...] = _mm(xs, wfv_ref)
    lf_ref[...] = _log_sigmoid(_mm(xs, wff_ref) + bff_ref[...])[:, :FOX_H]
    fq_ref[...] = _mm(xs, wfq_ref)
    lft_ref[...] = _log_sigmoid(_mm_nt(wfft_ref, xs)[:FOX_H] + bfft_ref[...])


def _proj1(x, w, prompt, tg, precise):
    rows = x.shape[0]
    tm = _row_tile(rows, 256)
    row = lambda n: pl.BlockSpec((tm, n), lambda i: (i, 0))
    common = ('wqkv', 'wa', 'wa2', 'ba', 'wr', 'wfkt' if prompt else 'wfk', 'wfvt' if prompt else 'wfv',
              'wff', 'bff', 'wfft', 'bfft')
    sds = lambda n, dt: jax.ShapeDtypeStruct((rows, n), dt)
    out_shape = [sds(1024, F32), sds(256, F32), sds(512, F32)]
    out_specs = [row(1024), row(256), row(512)]
    if prompt:
        assert tg % tm == 0
        tps = tg // tm
        names = common + ('wqa', 'bqa', 'wka', 'wva', 'bva')
        kern = functools.partial(_proj1_prompt_kernel, tiles_per_seq=tps, precise=precise)
        tr = lambda n: pl.BlockSpec((None, n, tm), lambda i: (i // tps, 0, i % tps))
        trs = lambda n: jax.ShapeDtypeStruct((rows // tg, n, tg), F32)
        n_aug = 6 if precise else 3
        out_shape += [trs(512), trs(512), trs(FOX_H)] + [sds(AUG_W, BF16)] * n_aug
        out_specs += [tr(512), tr(512), tr(FOX_H)] + [row(AUG_W)] * n_aug
        scratch = [pltpu.VMEM((SUBLANES, LANES), F32)]
    else:
        names = common + ('wfq',)
        kern = _proj1_sample_kernel
        out_shape += [sds(512, F32), sds(512, F32), sds(FOX_H, F32), sds(512, F32),
                      jax.ShapeDtypeStruct((FOX_H, rows), F32)]
        out_specs += [row(512), row(512), row(FOX_H), row(512), pl.BlockSpec((FOX_H, tm), lambda i: (0, i))]
        scratch = []
    ws = [w[n] for n in names]
    return pl.pallas_call(
        kern, grid=(rows // tm,),
        in_specs=[row(D_MODEL)] + [_const_spec(a.shape) for a in ws],
        out_specs=out_specs, out_shape=out_shape, scratch_shapes=scratch,
        compiler_params=_params('arbitrary'), name='proj1_prompt' if prompt else 'proj1_sample')(x, *ws)


def _proj2_kernel(x_ref, wc_ref, wg_ref, bch_ref, gate_ref, *, precise):
    xs = _pieces(x_ref[...], precise)
    bch_ref[...] = _mm(xs, wc_ref)
    gate_ref[...] = _mm(xs, wg_ref)


def _proj2(x, w, precise):
    rows = x.shape[0]
    tm = _row_tile(rows, 256)
    row = lambda n: pl.BlockSpec((tm, n), lambda i: (i, 0))
    return pl.pallas_call(
        functools.partial(_proj2_kernel, precise=precise), grid=(rows // tm,),
        in_specs=[row(D_MODEL), _const_spec(w['wc'].shape), _const_spec(w['wg'].shape)],
        out_specs=(row(3 * CONV_D), row(N_BRANCH * D_MODEL)),
        out_shape=(jax.ShapeDtypeStruct((rows, 3 * CONV_D), F32),
                   jax.ShapeDtypeStruct((rows, N_BRANCH * D_MODEL), F32)),
        compiler_params=_params('parallel'), name='proj2')(x, w['wc'], w['wg'])


def _gla_kernel(qkv_ref, la_ref, r_ref, s0_ref, g_ref, o_ref, sfin_ref, state, *, G, cr, cp, nchunk, precise):
    ti = pl.program_id(1)

    @pl.when(ti == 0)
    def _():
        state[...] = s0_ref[...]

    ri = lax.broadcasted_iota(jnp.int32, (cp, cp), 0)
    ci = lax.broadcasted_iota(jnp.int32, (cp, cp), 1)
    tril = ci <= ri
    lmat = jnp.where(tril, 1.0, 0.0).astype(BF16)
    ones = jnp.ones((cp, GLA_DV), BF16)
    gnorm = g_ref[...]

    def pad(a):
        if cp == cr:
            return a
        return jnp.concatenate([a, jnp.zeros((cp - cr, a.shape[1]), a.dtype)], axis=0)

    for g in range(G):
        def chunk(c, carry, g=g):
            r0 = pl.multiple_of(c * cr, cr)
            qkv = pad(qkv_ref[g, pl.ds(r0, cr), :])
            la = pad(la_ref[g, pl.ds(r0, cr), :])
            q = qkv[:, :GLA_H * GLA_DK]
            k = qkv[:, GLA_H * GLA_DK:2 * GLA_H * GLA_DK]
            v = qkv[:, 2 * GLA_H * GLA_DK:]
            la3 = _split3(la)
            b = _dot(lmat, la3[0]) + _dot(lmat, la3[1]) + _dot(lmat, la3[2])
            b_last = b[cp - 1:cp, :]
            q_in = q * jnp.exp(b)
            k_in = k * jnp.exp(-b)
            k_end = k * jnp.exp(b_last - b)
            outs = []
            for h in range(GLA_H):
                ks = slice(h * GLA_DK, (h + 1) * GLA_DK)
                vs = slice(h * GLA_DV, (h + 1) * GLA_DV)
                s_prev = state[g, h]
                att = jnp.where(tril, _pdot(_dot_nt, q_in[:, ks], k_in[:, ks], precise), 0.0)
                o = _pdot(_dot, att, v[:, vs], precise) + _pdot(_dot, q_in[:, ks], s_prev, precise)
                dlog = (_dot_tn(la3[0][:, ks], ones) + _dot_tn(la3[1][:, ks], ones)
                        + _dot_tn(la3[2][:, ks], ones))
                state[g, h] = jnp.exp(dlog) * s_prev + _pdot(_dot_tn, k_end[:, ks], v[:, vs], precise)
                ms = jnp.mean(o * o, axis=-1, keepdims=True)
                outs.append(o * lax.rsqrt(ms + LN_EPS) * gnorm[:, vs])
            o_all = jnp.concatenate(outs, axis=1)[:cr] * r_ref[g, pl.ds(r0, cr), :]
            o_ref[g, pl.ds(r0, cr), :] = o_all.astype(o_ref.dtype)
            return carry

        lax.fori_loop(0, nchunk, chunk, 0)

    @pl.when(ti == pl.num_programs(1) - 1)
    def _():
        sfin_ref[...] = state[...]


def _gla(gqkv, gla, gr, s0, gnorm, bg, tg, out_dtype, precise):
    if tg % GLA_CHUNK == 0:
        G, cr, cp, tt = 1, GLA_CHUNK, GLA_CHUNK, min(tg, 512)
    else:
        assert tg % SUBLANES == 0 and tg <= 2 * SUBLANES
        G, cr, cp, tt = SUBLANES, tg, 2 * SUBLANES, tg
    assert bg % G == 0 and tg % tt == 0
    blk = lambda n: pl.BlockSpec((G, tt, n), lambda b, t: (b, t, 0))
    sblk = pl.BlockSpec((G, GLA_H, GLA_DK, GLA_DV), lambda b, t: (b, 0, 0, 0))
    kern = functools.partial(_gla_kernel, G=G, cr=cr, cp=cp, nchunk=tt // cr, precise=precise)
    o, sfin = pl.pallas_call(
        kern, grid=(bg // G, tg // tt),
        in_specs=[blk(1024), blk(256), blk(512), sblk, _const_spec(gnorm.shape)],
        out_specs=(blk(512), sblk),
        out_shape=(jax.ShapeDtypeStruct((bg, tg, 512), out_dtype),
                   jax.ShapeDtypeStruct((bg, GLA_H, GLA_DK, GLA_DV), F32)),
        scratch_shapes=[pltpu.VMEM((G, GLA_H, GLA_DK, GLA_DV), F32)],
        compiler_params=_params('parallel', 'arbitrary'), name='gla')(
            gqkv.reshape(bg, tg, 1024), gla.reshape(bg, tg, 256), gr.reshape(bg, tg, 512), s0, gnorm)
    return o.reshape(bg * tg, 512), sfin


def _conv_kernel(bch_ref, buf_ref, w_ref, o_ref, st_ref, prev, *, G, tt):
    @pl.when(pl.program_id(1) == 0)
    def _():
        prev[...] = buf_ref[...]

    w = w_ref[...]
    rowi = lax.broadcasted_iota(jnp.int32, (tt, CONV_D), 0)
    for g in range(G):
        bch = bch_ref[g]
        u = bch[:, CONV_D:2 * CONV_D] * bch[:, 2 * CONV_D:]
        p = prev[g]
        u1 = jnp.where(rowi == 0, p[1:2], pltpu.roll(u, 1, 0))
        u2 = jnp.where(rowi == 0, p[0:1], jnp.where(rowi == 1, p[1:2], pltpu.roll(u, 2, 0)))
        y = w[0:1] * u2 + w[1:2] * u1 + w[2:3] * u
        o_ref[g] = (bch[:, :CONV_D] * y).astype(o_ref.dtype)
        tail = u[tt - (CONV_W - 1):tt]
        prev[g] = tail
        st_ref[g] = tail


def _conv(bch, buf, conv_w, bg, tg, out_dtype):
    if tg >= 512:
        G, tt = 1, 512
    else:
        G, tt = min(bg, 16), tg
    assert bg % G == 0 and tg % tt == 0 and tt >= CONV_W - 1
    sblk = pl.BlockSpec((G, CONV_W - 1, CONV_D), lambda b, t: (b, 0, 0))
    o, st = pl.pallas_call(
        functools.partial(_conv_kernel, G=G, tt=tt), grid=(bg // G, tg // tt),
        in_specs=[pl.BlockSpec((G, tt, 3 * CONV_D), lambda b, t: (b, t, 0)), sblk,
                  _const_spec(conv_w.shape)],
        out_specs=(pl.BlockSpec((G, tt, CONV_D), lambda b, t: (b, t, 0)), sblk),
        out_shape=(jax.ShapeDtypeStruct((bg, tg, CONV_D), out_dtype),
                   jax.ShapeDtypeStruct((bg, CONV_W - 1, CONV_D), F32)),
        scratch_shapes=[pltpu.VMEM((G, CONV_W - 1, CONV_D), F32)],
        compiler_params=_params('parallel', 'arbitrary'), name='conv')(
            bch.reshape(bg, tg, 3 * CONV_D), buf, conv_w)
    return o.reshape(bg * tg, CONV_D), st


def _fox_prompt_kernel(*refs, tq, tk, hb, precise, q_off):
    if precise:
        q_ref, k_ref, v_ref, ql_ref, kl_ref, vl_ref, o_ref, m_scr, acc_scr = refs
    else:
        q_ref, k_ref, v_ref, o_ref, m_scr, acc_scr = refs
    qi = pl.program_id(2) + q_off
    nfull = (qi * tq) // tk
    row = lax.broadcasted_iota(jnp.int32, (tq, tk), 0)
    col = lax.broadcasted_iota(jnp.int32, (tq, tk), 1)
    m_scr[...] = jnp.full_like(m_scr, -jnp.inf)
    acc_scr[...] = jnp.zeros_like(acc_scr)

    def update(k0, masked):
        for h in range(hb):
            ls = slice(h * AUG, (h + 1) * AUG)
            kt = k_ref[pl.ds(k0, tk), ls]
            s = _dot_nt(q_ref[:, ls], kt)
            if precise:
                s = s + _dot_nt(ql_ref[:, ls], kt) + _dot_nt(q_ref[:, ls], kl_ref[pl.ds(k0, tk), ls])
            if masked:
                s = jnp.where(col + k0 <= row + qi * tq, s, -jnp.inf)
            m_prev = m_scr[h]
            m_new = jnp.maximum(m_prev, jnp.max(s, axis=-1, keepdims=True))
            alpha = jnp.exp(m_prev - m_new)
            p = jnp.exp(s - jnp.concatenate([m_new] * (tk // LANES), axis=1))
            pb = p.astype(BF16)
            vt = v_ref[pl.ds(k0, tk), ls]
            pv = _dot(pb, vt)
            if precise:
                pv = pv + _dot((p - pb.astype(F32)).astype(BF16), vt) + _dot(pb, vl_ref[pl.ds(k0, tk), ls])
            acc_scr[h] = alpha * acc_scr[h] + pv
            m_scr[h] = m_new

    def body(kj, carry):
        update(pl.multiple_of(kj * tk, tk), False)
        return carry

    lax.fori_loop(0, nfull, body, 0)
    update(pl.multiple_of(nfull * tk, tk), True)
    lane = lax.broadcasted_iota(jnp.int32, (1, AUG), 1)
    outs = []
    for h in range(0, hb, 2):
        o = []
        for hh in (h, h + 1):
            acc = acc_scr[hh]
            o.append(acc / acc[:, FOX_HD:FOX_HD + 1])
        outs.append(jnp.where(lane < FOX_HD, o[0], pltpu.roll(o[1], FOX_HD, 1)))
    o_ref[...] = jnp.concatenate(outs, axis=1).astype(o_ref.dtype)


def _fox_prompt(qkv, bg, tg, n_run=None):
    tq = min(tg, 256)
    tk = min(tg, 512)
    hb = 4
    nq = tg // tq
    precise = len(qkv) == 6
    n_run = nq if n_run is None else n_run
    q_spec = pl.BlockSpec((tq, hb * AUG), lambda b, p, i: (b * nq + (nq - n_run) + i, p))
    kv_spec = pl.BlockSpec((tg, hb * AUG), lambda b, p, i: (b, p), pipeline_mode=pl.Buffered(1))
    specs = [q_spec, kv_spec, kv_spec]
    return pl.pallas_call(
        functools.partial(_fox_prompt_kernel, tq=tq, tk=tk, hb=hb, precise=precise, q_off=nq - n_run),
        grid=(bg, FOX_H // hb, n_run),
        in_specs=specs * (2 if precise else 1),
        out_specs=pl.BlockSpec((tq, hb * FOX_HD), lambda b, p, i: (b * n_run + i, p)),
        out_shape=jax.ShapeDtypeStruct((bg * n_run * tq, BRANCH_W), F32 if precise else BF16),
        scratch_shapes=[pltpu.VMEM((hb, tq, AUG), F32)] * 2,
        compiler_params=_params('parallel', 'parallel', 'arbitrary'), name='fox_prompt')(*qkv)


def _fox_sample_kernel(pt_ref, q_ref, kn_ref, vn_ref, lfn_ref, *refs, P, tq):
    del pt_ref
    k_refs, v_refs, lf_refs = refs[:P], refs[P:2 * P], refs[2 * P:3 * P]
    o_ref, qbd, m_scr, l_scr, acc_scr, carry = refs[3 * P:]
    j = pl.program_id(1)
    nrow = tq * FOX_H
    hrow = lax.broadcasted_iota(jnp.int32, (FOX_H, BRANCH_W), 0)
    hcol = lax.broadcasted_iota(jnp.int32, (FOX_H, BRANCH_W), 1)
    hmask = (hcol // FOX_HD) == hrow

    @pl.when(j == 0)
    def _():
        q = q_ref[...]
        qbd[...] = jnp.concatenate(
            [jnp.where(hmask, jnp.broadcast_to(q[t:t + 1, :], (FOX_H, BRANCH_W)), 0.0) for t in range(tq)],
            axis=0).astype(BF16)
        m_scr[...] = jnp.full_like(m_scr, -jnp.inf)
        l_scr[...] = jnp.zeros_like(l_scr)
        acc_scr[...] = jnp.zeros_like(acc_scr)
        carry[...] = jnp.zeros_like(carry)

    def update(qk, lf, causal, pv):
        pre, tot = _lane_prefix(lf)
        c = carry[...] + pre
        carry[...] = carry[...] + tot
        s = qk - jnp.concatenate([c] * tq, axis=0)
        if causal:
            key = lax.broadcasted_iota(jnp.int32, (nrow, PAGE_SIZE), 1)
            tok = lax.broadcasted_iota(jnp.int32, (nrow, PAGE_SIZE), 0) // FOX_H
            s = jnp.where(key <= tok, s, -jnp.inf)
        m_prev = m_scr[...]
        m_new = jnp.maximum(m_prev, jnp.max(s, axis=-1, keepdims=True))
        alpha = jnp.exp(m_prev - m_new)
        p = jnp.exp(s - m_new)
        l_scr[...] = alpha * l_scr[...] + jnp.sum(p, axis=-1, keepdims=True)
        acc_scr[...] = (jnp.concatenate([alpha] * (BRANCH_W // LANES), axis=1) * acc_scr[...]
                        + pv(p.astype(BF16)))
        m_scr[...] = m_new

    for i in range(P):
        kt = k_refs[i][...].reshape(BRANCH_W, PAGE_SIZE).astype(BF16)
        vt = v_refs[i][...].reshape(BRANCH_W, PAGE_SIZE).astype(BF16)
        update(_dot(qbd[...], kt), lf_refs[i][...], False, lambda p, vt=vt: _dot_nt(p, vt))

    @pl.when(j == pl.num_programs(1) - 1)
    def _():
        zpad = jnp.zeros((PAGE_SIZE - tq, BRANCH_W), F32)
        kn = jnp.concatenate([kn_ref[...], zpad], axis=0).astype(BF16)
        vn = jnp.concatenate([vn_ref[...], zpad], axis=0).astype(BF16)
        update(_dot_nt(qbd[...], kn), lfn_ref[0], True, lambda p: _dot(p, vn))
        o = acc_scr[...] / jnp.concatenate([l_scr[...]] * (BRANCH_W // LANES), axis=1)
        rows = [jnp.sum(jnp.where(hmask, o[t * FOX_H:(t + 1) * FOX_H], 0.0), axis=0, keepdims=True)
                for t in range(tq)]
        o_ref[...] = jnp.concatenate(rows, axis=0)


def _fox_sample(fq, fk32, fv32, lft, layer, k_cache, v_cache, lf_cache, page_table, db, tq):
    n_pages = page_table.shape[1]
    P = 8 if n_pages % 8 == 0 else (4 if n_pages % 4 == 0 else 1)
    assert tq == SUBLANES
    kt = jnp.transpose(k_cache, (0, 1, 3, 4, 2))
    vt = jnp.transpose(v_cache, (0, 1, 3, 4, 2))
    lfp = jnp.swapaxes(lf_cache, 2, 3)
    lfn = jnp.swapaxes(lft.reshape(FOX_H, db, tq), 0, 1)
    lfn = jnp.pad(lfn, ((0, 0), (0, 0), (0, PAGE_SIZE - tq)))
    tok = lambda: pl.BlockSpec((tq, BRANCH_W), lambda b, j, pt: (b, 0))
    page = lambda i: pl.BlockSpec((None, None, FOX_H, FOX_HD, PAGE_SIZE),
                                  lambda b, j, pt, i=i: (layer, pt[b, j * P + i], 0, 0, 0))
    lfpage = lambda i: pl.BlockSpec((None, None, FOX_H, PAGE_SIZE),
                                    lambda b, j, pt, i=i: (layer, pt[b, j * P + i], 0, 0))
    nrow = tq * FOX_H
    return pl.pallas_call(
        functools.partial(_fox_sample_kernel, P=P, tq=tq),
        grid_spec=pltpu.PrefetchScalarGridSpec(
            num_scalar_prefetch=1, grid=(db, n_pages // P),
            in_specs=[tok(), tok(), tok(), pl.BlockSpec((1, FOX_H, PAGE_SIZE), lambda b, j, pt: (b, 0, 0))]
                     + [page(i) for i in range(P)] + [page(i) for i in range(P)]
                     + [lfpage(i) for i in range(P)],
            out_specs=tok(),
            scratch_shapes=[pltpu.VMEM((nrow, BRANCH_W), BF16), pltpu.VMEM((nrow, LANES), F32),
                            pltpu.VMEM((nrow, LANES), F32), pltpu.VMEM((nrow, BRANCH_W), F32),
                            pltpu.VMEM((FOX_H, LANES), F32)]),
        out_shape=jax.ShapeDtypeStruct((db * tq, BRANCH_W), F32),
        compiler_params=_params('parallel', 'arbitrary'), name='fox_sample')(
            page_table, fq, fk32, fv32, lfn, *([kt] * P), *([vt] * P), *([lfp] * P))


def _merge_kernel(x_ref, og_ref, of_ref, oc_ref, gate_ref, wb_ref, wo_ref, g_ref, b_ref, y_ref, *, precise):
    m = None
    for jb, o_ref in enumerate((og_ref, of_ref, oc_ref)):
        gj = _sigmoid(gate_ref[:, jb * D_MODEL:(jb + 1) * D_MODEL])
        term = gj * _mm(_pieces(o_ref[...].astype(F32), precise), wb_ref.at[:, jb])
        m = term if m is None else m + term
    y = DEEPNORM_ALPHA * x_ref[...] + _mm(_pieces(m, precise), wo_ref)
    y_ref[...] = _layernorm(y, g_ref[...], b_ref[...])


def _merge(x, o_gla, o_fox, o_conv, gate, w, precise):
    rows = x.shape[0]
    tm = _row_tile(rows, 256)
    row = lambda n: pl.BlockSpec((tm, n), lambda i: (i, 0))
    consts = [w['wb'], w['wo'], w['ln1_g'], w['ln1_b']]
    return pl.pallas_call(
        functools.partial(_merge_kernel, precise=precise), grid=(rows // tm,),
        in_specs=[row(D_MODEL), row(BRANCH_W), row(BRANCH_W), row(BRANCH_W), row(N_BRANCH * D_MODEL)]
                 + [_const_spec(a.shape) for a in consts],
        out_specs=row(D_MODEL), out_shape=jax.ShapeDtypeStruct((rows, D_MODEL), F32),
        compiler_params=_params('parallel'), name='merge')(x, o_gla, o_fox, o_conv, gate, *consts)


def _route(probs):
    rows = [probs[i:i + 1, :] for i in range(N_EXPERTS)]
    gscore = []
    for g in range(N_GROUPS):
        r0, r1, r2, r3 = rows[EXP_PER_GROUP * g:EXP_PER_GROUP * (g + 1)]
        a, b = jnp.maximum(r0, r1), jnp.minimum(r0, r1)
        c, d = jnp.maximum(r2, r3), jnp.minimum(r2, r3)
        gscore.append(jnp.maximum(a, c) + jnp.maximum(jnp.minimum(a, c), jnp.maximum(b, d)))
    best, gsel = gscore[0], jnp.zeros_like(gscore[0], dtype=jnp.int32)
    for g in range(1, N_GROUPS):
        upd = gscore[g] > best
        best = jnp.where(upd, gscore[g], best)
        gsel = jnp.where(upd, g, gsel)
    v = []
    for i in range(EXP_PER_GROUP):
        vi = rows[(N_GROUPS - 1) * EXP_PER_GROUP + i]
        for g in range(N_GROUPS - 2, -1, -1):
            vi = jnp.where(gsel == g, rows[g * EXP_PER_GROUP + i], vi)
        v.append(vi)
    v1, i1 = v[0], jnp.zeros_like(gsel)
    for i in range(1, EXP_PER_GROUP):
        upd = v[i] > v1
        v1 = jnp.where(upd, v[i], v1)
        i1 = jnp.where(upd, i, i1)
    v2, i2 = jnp.full_like(v1, -1.0), jnp.full_like(gsel, -1)
    for i in range(EXP_PER_GROUP):
        cand = jnp.where(i1 == i, -1.0, v[i])
        upd = cand > v2
        v2 = jnp.where(upd, cand, v2)
        i2 = jnp.where(upd, i, i2)
    den = v1 + v2
    w1, w2 = v1 / den, v2 / den
    out = []
    for g in range(N_GROUPS):
        for i in range(EXP_PER_GROUP):
            wi = jnp.where(i1 == i, w1, jnp.where(i2 == i, w2, 0.0))
            out.append(jnp.where(gsel == g, wi, 0.0))
    return out


def _moe_kernel(x_ref, wrt_ref, br_ref, wg_ref, wu_ref, wd_ref, g_ref, b_ref, y_ref):
    x = x_ref[...]
    tm = x.shape[0]
    xb = x.astype(BF16)
    x_lo = (x - xb.astype(F32)).astype(BF16)
    wrt = wrt_ref[...]
    w_hi = wrt.astype(BF16)
    w_lo = (wrt - w_hi.astype(F32)).astype(BF16)
    logits = _dot_nt(w_hi, xb) + _dot_nt(w_hi, x_lo) + _dot_nt(w_lo, xb) + br_ref[...]
    e = jnp.exp(logits - jnp.max(logits, axis=0, keepdims=True))
    probs = e / jnp.sum(e, axis=0, keepdims=True)
    gate_rows = _route(probs)
    gate_t = jnp.concatenate(gate_rows + [jnp.zeros((LANES - N_EXPERTS, tm), F32)], axis=0)
    gate = gate_t.T
    acc = jnp.zeros((tm, D_MODEL), F32)
    for ex in range(N_EXPERTS):
        hg = _dot(xb, wg_ref[ex])
        hu = _dot(xb, wu_ref[ex])
        h = (hg * _sigmoid(hg)) * hu * gate[:, ex:ex + 1]
        acc = acc + _dot(h.astype(BF16), wd_ref[ex])
    y = DEEPNORM_ALPHA * x + acc
    y_ref[...] = _layernorm(y, g_ref[...], b_ref[...])


def _moe(x, w):
    rows = x.shape[0]
    tm = _row_tile(rows, 512)
    row = pl.BlockSpec((tm, D_MODEL), lambda i: (i, 0))
    consts = [w['wrt'], w['br'], w['weg'], w['weu'], w['wed'], w['ln2_g'], w['ln2_b']]
    return pl.pallas_call(
        _moe_kernel, grid=(rows // tm,),
        in_specs=[row] + [_const_spec(a.shape) for a in consts],
        out_specs=row, out_shape=jax.ShapeDtypeStruct((rows, D_MODEL), F32),
        compiler_params=_params('parallel'), name='moe')(x, *consts)


def _prep_weights(l, precise, w_in, w_gla_a2, b_gla_a, gla_norm_g, b_fox_f, conv_w, w_branch, w_out, ln1_g, ln1_b,
                  w_router, b_router, w_e_gate, w_e_up, w_e_down, ln2_g, ln2_b):
    wi = w_in[l]
    off = {}
    o = 0
    for name, n in (('gla_q', 256), ('gla_k', 256), ('gla_v', 512), ('gla_a', GLA_LR), ('gla_r', 512),
                    ('fox_q', 512), ('fox_k', 512), ('fox_v', 512), ('fox_f', FOX_H),
                    ('conv', 3 * CONV_D), ('gate', N_BRANCH * D_MODEL)):
        off[name] = (o, o + n)
        o += n
    col = lambda name: wi[:, off[name][0]:off[name][1]]
    padc = lambda a, n: jnp.pad(a, ((0, 0), (0, n - a.shape[1])))
    row2 = lambda a: a.reshape(1, -1)

    def bf(a):
        hi = a.astype(BF16)
        return jnp.stack([hi, (a - hi.astype(F32)).astype(BF16)]) if precise else hi[None]

    ff = col('fox_f')
    aug = lambda a: jnp.pad(a.reshape(D_MODEL, FOX_H, FOX_HD), ((0, 0), (0, 0), (0, AUG - FOX_HD))).reshape(D_MODEL, AUG_W)
    aug_bias = lambda n: jnp.tile(jnp.pad(jnp.ones((n,), F32), (FOX_HD, AUG - FOX_HD - n)), FOX_H).reshape(1, AUG_W)
    return {
        'wqkv': bf(jnp.concatenate([col('gla_q') * (GLA_DK ** -0.5), col('gla_k'), col('gla_v')], axis=1)),
        'wa': bf(padc(col('gla_a'), LANES)),
        'wa2': bf(jnp.pad(w_gla_a2[l], ((0, LANES - GLA_LR), (0, 0)))),
        'ba': row2(b_gla_a[l]),
        'wr': bf(col('gla_r')),
        'wfq': bf((col('fox_q') * (FOX_HD ** -0.5))),
        'wfk': bf(col('fox_k')),
        'wfv': bf(col('fox_v')),
        'wfkt': bf(col('fox_k').T),
        'wfvt': bf(col('fox_v').T),
        'wff': bf(padc(ff, LANES)),
        'bff': padc(row2(b_fox_f[l]), LANES),
        'wfft': bf(jnp.pad(ff.T, ((0, 2 * SUBLANES - FOX_H), (0, 0)))),
        'bfft': b_fox_f[l].reshape(FOX_H, 1),
        'wqa': bf(aug(col('fox_q') * (FOX_HD ** -0.5))),
        'bqa': aug_bias(3),
        'wka': bf(aug(col('fox_k'))),
        'wva': bf(aug(col('fox_v'))),
        'bva': aug_bias(1),
        'wc': bf(col('conv')),
        'wg': bf(col('gate')),
        'gnorm': row2(gla_norm_g[l]),
        'conv_w': conv_w[l],
        'wb': bf(w_branch[l]),
        'wo': bf(w_out[l]),
        'ln1_g': row2(ln1_g[l]), 'ln1_b': row2(ln1_b[l]),
        'wrt': w_router.T, 'br': b_router.reshape(N_EXPERTS, 1),
        'weg': w_e_gate[l].astype(BF16), 'weu': w_e_up[l].astype(BF16), 'wed': w_e_down[l].astype(BF16),
        'ln2_g': row2(ln2_g[l]), 'ln2_b': row2(ln2_b[l]),
    }


TAIL = 256
SPLICE = 128


def _layer(x, w, bg, tg, gla_s0, conv_buf, pools, act_dtype, precise):
    prompt = pools is None
    precise = precise and prompt and tg >= 2 * TAIL
    gqkv, gla, gr, fk, fv, lf, *fox_in = _proj1(x, w, prompt, tg, precise)
    bch, gate = _proj2(x, w, False)
    o_gla, gla_state = _gla(gqkv, gla, gr, gla_s0, w['gnorm'], bg, tg, F32 if precise else act_dtype, precise)
    if prompt:
        o_fox = _fox_prompt(fox_in[:3], bg, tg)
        caches = (fk.reshape(bg, FOX_H, FOX_HD, tg), fv.reshape(bg, FOX_H, FOX_HD, tg), lf)
    else:
        o_fox = _fox_sample(fox_in[0], fk, fv, fox_in[1], *pools, bg, tg)
        caches = (fk.reshape(bg, tg, FOX_H, FOX_HD), fv.reshape(bg, tg, FOX_H, FOX_HD), lf.reshape(bg, tg, FOX_H))
    o_conv, conv_state = _conv(bch, conv_buf, w['conv_w'], bg, tg, act_dtype)
    y = _moe(_merge(x, o_gla, o_fox, o_conv, gate, w, False), w)
    if precise:
        n_t = TAIL // min(tg, 256)
        tail = lambda a: a.reshape(bg, tg, a.shape[-1])[:, tg - TAIL:].reshape(bg * TAIL, a.shape[-1])
        xt = tail(x)
        bch_t, gate_t = _proj2(xt, w, True)
        o_fox_t = _fox_prompt(fox_in, bg, tg, n_run=n_t)
        o_conv_t, _ = _conv(bch_t, jnp.zeros_like(conv_buf), w['conv_w'], bg, TAIL, F32)
        y_t = _moe(_merge(xt, tail(o_gla), o_fox_t, o_conv_t, gate_t, w, True), w)
        y = y.reshape(bg, tg, D_MODEL).at[:, tg - SPLICE:].set(
            y_t.reshape(bg, TAIL, D_MODEL)[:, TAIL - SPLICE:]).reshape(bg * tg, D_MODEL)
    return y, caches + (gla_state, conv_state)


def kernel(x_prompt, x_sample, cache_fox_k, cache_fox_v, cache_fox_logf, state_gla, state_conv, page_table, w_in, w_gla_a2, b_gla_a, gla_norm_g, b_fox_f, conv_w, w_branch, w_out, ln1_g, ln1_b, w_router, b_router, w_e_gate, w_e_up, w_e_down, ln2_g, ln2_b):
    B, T, _ = x_prompt.shape
    DB, TS, _ = x_sample.shape
    xp = x_prompt.reshape(B * T, D_MODEL)
    xs = x_sample.reshape(DB * TS, D_MODEL)
    depth = w_in.shape[0]
    st_p, st_s = [], []
    for l in range(depth):
        precise = l < depth - 1
        w = _prep_weights(l, precise, w_in, w_gla_a2, b_gla_a, gla_norm_g, b_fox_f, conv_w, w_branch, w_out,
                          ln1_g, ln1_b, w_router, b_router, w_e_gate, w_e_up, w_e_down, ln2_g, ln2_b)
        xp, sp = _layer(xp, w, B, T, jnp.zeros((B, GLA_H, GLA_DK, GLA_DV), F32),
                        jnp.zeros((B, CONV_W - 1, CONV_D), F32), None, BF16, precise)
        xs, ss = _layer(xs, w, DB, TS, state_gla[l], state_conv[l],
                        (l, cache_fox_k, cache_fox_v, cache_fox_logf, page_table), F32, False)
        st_p.append(sp)
        st_s.append(ss)
    stack = lambda sts, i: jnp.stack([s[i] for s in sts])
    return (xp.reshape(B, T, D_MODEL), xs.reshape(DB, TS, D_MODEL),
            jnp.transpose(stack(st_p, 0), (0, 1, 4, 2, 3)), jnp.transpose(stack(st_p, 1), (0, 1, 4, 2, 3)),
            jnp.swapaxes(stack(st_p, 2), 2, 3), stack(st_p, 3), stack(st_p, 4),
            stack(st_s, 0), stack(st_s, 1), stack(st_s, 2), stack(st_s, 3), stack(st_s, 4))
```

```python
import functools

import jax
import jax.numpy as jnp
from jax import lax
from jax.experimental import pallas as pl
from jax.experimental.pallas import tpu as pltpu

F32 = jnp.float32
BF16 = jnp.bfloat16

D_MODEL = 1024
DEPTH = 2
PAGE_SIZE = 128
BRANCH_W = D_MODEL // 2
GLA_H = 4
GLA_DK = BRANCH_W // (2 * GLA_H)
GLA_DV = BRANCH_W // GLA_H
GLA_LR = 16
GLA_TAU = 16.0
GLA_CHUNK = 64
FOX_HD = 64
FOX_H = BRANCH_W // FOX_HD
CONV_D = BRANCH_W
CONV_W = 3
N_BRANCH = 3
N_EXPERTS = 16
N_GROUPS = 4
EXP_PER_GROUP = N_EXPERTS // N_GROUPS
D_EXPERT = D_MODEL // 4
LN_EPS = 1e-5
DEEPNORM_ALPHA = (2 * DEPTH) ** 0.25

LANES = 128
SUBLANES = 8
VMEM_LIMIT = 56 * 1024 * 1024

_NT = (((1,), (1,)), ((), ()))
_TN = (((0,), (0,)), ((), ()))


def _dot(a, b):
    return jnp.dot(a, b, preferred_element_type=F32)


def _dot_nt(a, b):
    return lax.dot_general(a, b, _NT, preferred_element_type=F32)


def _dot_tn(a, b):
    return lax.dot_general(a, b, _TN, preferred_element_type=F32)


def _split3(x):
    hi = x.astype(BF16)
    r1 = x - hi.astype(F32)
    mid = r1.astype(BF16)
    lo = (r1 - mid.astype(F32)).astype(BF16)
    return hi, mid, lo


def _pieces(x, precise):
    hi = x.astype(BF16)
    if not precise:
        return (hi,)
    return hi, (x - hi.astype(F32)).astype(BF16)


def _mm(xs, w_ref):
    y = _dot(xs[0], w_ref[0])
    if len(xs) == 2:
        y = y + _dot(xs[1], w_ref[0]) + _dot(xs[0], w_ref[1])
    return y


def _mm_nt(w_ref, xs):
    y = _dot_nt(w_ref[0], xs[0])
    if len(xs) == 2:
        y = y + _dot_nt(w_ref[0], xs[1]) + _dot_nt(w_ref[1], xs[0])
    return y


def _pdot(dot, a, b, precise):
    ap, bp = _pieces(a, precise), _pieces(b, precise)
    y = dot(ap[0], bp[0])
    if precise:
        y = y + dot(ap[1], bp[0]) + dot(ap[0], bp[1])
    return y


def _log_sigmoid(x):
    return jnp.minimum(x, 0.0) - jnp.log1p(jnp.exp(-jnp.abs(x)))


def _sigmoid(x):
    return 1.0 / (1.0 + jnp.exp(-x))


def _layernorm(y, g, b):
    mu = jnp.mean(y, axis=-1, keepdims=True)
    d = y - mu
    var = jnp.mean(d * d, axis=-1, keepdims=True)
    return d * lax.rsqrt(var + LN_EPS) * g + b


def _lane_prefix(x):
    r = lax.broadcasted_iota(jnp.int32, (LANES, 2 * LANES), 0)
    c = lax.broadcasted_iota(jnp.int32, (LANES, 2 * LANES), 1)
    u = jnp.where(r <= c, 1.0, 0.0).astype(BF16)
    hi, mid, lo = _split3(x)
    y = _dot(hi, u) + _dot(mid, u) + _dot(lo, u)
    return y[:, :LANES], y[:, LANES:]


def _const_spec(shape):
    nd = len(shape)
    return pl.BlockSpec(shape, lambda *_: (0,) * nd, pipeline_mode=pl.Buffered(1))


def _params(*sem):
    return pltpu.CompilerParams(dimension_semantics=sem, vmem_limit_bytes=VMEM_LIMIT)


def _row_tile(rows, want):
    t = min(rows, want)
    assert rows % t == 0
    return t


AUG = 2 * FOX_HD
AUG_W = FOX_H * AUG


def _gla_cols(xs, wqkv_ref, wa_ref, wa2_ref, ba_ref, wr_ref, gqkv_ref, gla_ref, gr_ref):
    gqkv_ref[...] = _mm(xs, wqkv_ref)
    a_pre = _mm(_pieces(_mm(xs, wa_ref), len(xs) == 2), wa2_ref) + ba_ref[...]
    gla_ref[...] = _log_sigmoid(a_pre) * (1.0 / GLA_TAU)
    r = _mm(xs, wr_ref)
    gr_ref[...] = r * _sigmoid(r)


def _proj1_prompt_kernel(x_ref, wqkv_ref, wa_ref, wa2_ref, ba_ref, wr_ref, wfk_ref, wfv_ref, wff_ref, bff_ref,
                         wfft_ref, bfft_ref, wqa_ref, bqa_ref, wka_ref, wva_ref, bva_ref,
                         gqkv_ref, gla_ref, gr_ref, fk_ref, fv_ref, lf_ref, qa_ref, ka_ref, va_ref,
                         *rest, tiles_per_seq, precise):
    tm = x_ref.shape[0]
    lo_refs, carry = rest[:-1], rest[-1]

    @pl.when(pl.program_id(0) % tiles_per_seq == 0)
    def _():
        carry[...] = jnp.zeros_like(carry)

    xs = _pieces(x_ref[...], precise)
    _gla_cols(xs, wqkv_ref, wa_ref, wa2_ref, ba_ref, wr_ref, gqkv_ref, gla_ref, gr_ref)
    fk_ref[...] = _mm_nt(wfk_ref, xs)
    fv_ref[...] = _mm_nt(wfv_ref, xs)
    lf_ref[...] = _log_sigmoid(_mm_nt(wfft_ref, xs)[:FOX_H] + bfft_ref[...])
    lf = _log_sigmoid(_mm(xs, wff_ref) + bff_ref[...])
    ri = lax.broadcasted_iota(jnp.int32, (tm, tm), 0)
    ci = lax.broadcasted_iota(jnp.int32, (tm, tm), 1)
    lmat = jnp.where(ci <= ri, 1.0, 0.0).astype(BF16)
    hi, mid, lo = _split3(lf)
    c = _dot(lmat, hi) + _dot(lmat, mid) + _dot(lmat, lo) + carry[0:1, :]
    carry[...] = jnp.broadcast_to(c[tm - 1:tm, :], carry.shape)
    pr = lax.broadcasted_iota(jnp.int32, (LANES, AUG_W), 0)
    pc = lax.broadcasted_iota(jnp.int32, (LANES, AUG_W), 1)
    ka = _mm(xs, wka_ref)
    for j, part in enumerate(_split3(-c)):
        place = jnp.where(pc == pr * AUG + (FOX_HD + j), 1.0, 0.0).astype(BF16)
        ka = ka + _dot(part, place)
    qa = _mm(xs, wqa_ref) + bqa_ref[...]
    va = _mm(xs, wva_ref) + bva_ref[...]
    for i, (val, hi_ref) in enumerate(((qa, qa_ref), (ka, ka_ref), (va, va_ref))):
        hi = val.astype(BF16)
        hi_ref[...] = hi
        if precise:
            lo_refs[i][...] = (val - hi.astype(F32)).astype(BF16)


def _proj1_sample_kernel(x_ref, wqkv_ref, wa_ref, wa2_ref, ba_ref, wr_ref, wfk_ref, wfv_ref, wff_ref, bff_ref,
                         wfft_ref, bfft_ref, wfq_ref,
                         gqkv_ref, gla_ref, gr_ref, fk_ref, fv_ref, lf_ref, fq_ref, lft_ref):
    xs = _pieces(x_ref[...], False)
    _gla_cols(xs, wqkv_ref, wa_ref, wa2_ref, ba_ref, wr_ref, gqkv_ref, gla_ref, gr_ref)
    fk_ref[...] = _mm(xs, wfk_ref)
    fv_ref[...] = _mm(xs, wfv_ref)
    lf_ref[...] = _log_sigmoid(_mm(xs, wff_ref) + bff_ref[...])[:, :FOX_H]
    fq_ref[...] = _mm(xs, wfq_ref)
    lft_ref[...] = _log_sigmoid(_mm_nt(wfft_ref, xs)[:FOX_H] + bfft_ref[...])


def _proj1(x, w, prompt, tg, precise):
    rows = x.shape[0]
    tm = _row_tile(rows, 256)
    row = lambda n: pl.BlockSpec((tm, n), lambda i: (i, 0))
    common = ('wqkv', 'wa', 'wa2', 'ba', 'wr', 'wfkt' if prompt else 'wfk', 'wfvt' if prompt else 'wfv',
              'wff', 'bff', 'wfft', 'bfft')
    sds = lambda n, dt: jax.ShapeDtypeStruct((rows, n), dt)
    out_shape = [sds(1024, F32), sds(256, F32), sds(512, F32)]
    out_specs = [row(1024), row(256), row(512)]
    if prompt:
        assert tg % tm == 0
        tps = tg // tm
        names = common + ('wqa', 'bqa', 'wka', 'wva', 'bva')
        kern = functools.partial(_proj1_prompt_kernel, tiles_per_seq=tps, precise=precise)
        tr = lambda n: pl.BlockSpec((None, n, tm), lambda i: (i // tps, 0, i % tps))
        trs = lambda n: jax.ShapeDtypeStruct((rows // tg, n, tg), F32)
        n_aug = 6 if precise else 3
        out_shape += [trs(512), trs(512), trs(FOX_H)] + [sds(AUG_W, BF16)] * n_aug
        out_specs += [tr(512), tr(512), tr(FOX_H)] + [row(AUG_W)] * n_aug
        scratch = [pltpu.VMEM((SUBLANES, LANES), F32)]
    else:
        names = common + ('wfq',)
        kern = _proj1_sample_kernel
        out_shape += [sds(512, F32), sds(512, F32), sds(FOX_H, F32), sds(512, F32),
                      jax.ShapeDtypeStruct((FOX_H, rows), F32)]
        out_specs += [row(512), row(512), row(FOX_H), row(512), pl.BlockSpec((FOX_H, tm), lambda i: (0, i))]
        scratch = []
    ws = [w[n] for n in names]
    return pl.pallas_call(
        kern, grid=(rows // tm,),
        in_specs=[row(D_MODEL)] + [_const_spec(a.shape) for a in ws],
        out_specs=out_specs, out_shape=out_shape, scratch_shapes=scratch,
        compiler_params=_params('arbitrary'), name='proj1_prompt' if prompt else 'proj1_sample')(x, *ws)


def _proj2_kernel(x_ref, wc_ref, wg_ref, bch_ref, gate_ref, *, precise):
    xs = _pieces(x_ref[...], precise)
    bch_ref[...] = _mm(xs, wc_ref)
    gate_ref[...] = _mm(xs, wg_ref)


def _proj2(x, w, precise):
    rows = x.shape[0]
    tm = _row_tile(rows, 256)
    row = lambda n: pl.BlockSpec((tm, n), lambda i: (i, 0))
    return pl.pallas_call(
        functools.partial(_proj2_kernel, precise=precise), grid=(rows // tm,),
        in_specs=[row(D_MODEL), _const_spec(w['wc'].shape), _const_spec(w['wg'].shape)],
        out_specs=(row(3 * CONV_D), row(N_BRANCH * D_MODEL)),
        out_shape=(jax.ShapeDtypeStruct((rows, 3 * CONV_D), F32),
                   jax.ShapeDtypeStruct((rows, N_BRANCH * D_MODEL), F32)),
        compiler_params=_params('parallel'), name='proj2')(x, w['wc'], w['wg'])


def _gla_kernel(qkv_ref, la_ref, r_ref, s0_ref, g_ref, o_ref, sfin_ref, state, *, G, cr, cp, nchunk, precise):
    ti = pl.program_id(1)

    @pl.when(ti == 0)
    def _():
        state[...] = s0_ref[...]

    ri = lax.broadcasted_iota(jnp.int32, (cp, cp), 0)
    ci = lax.broadcasted_iota(jnp.int32, (cp, cp), 1)
    tril = ci <= ri
    lmat = jnp.where(tril, 1.0, 0.0).astype(BF16)
    ones = jnp.ones((cp, GLA_DV), BF16)
    gnorm = g_ref[...]

    def pad(a):
        if cp == cr:
            return a
        return jnp.concatenate([a, jnp.zeros((cp - cr, a.shape[1]), a.dtype)], axis=0)

    for g in range(G):
        def chunk(c, carry, g=g):
            r0 = pl.multiple_of(c * cr, cr)
            qkv = pad(qkv_ref[g, pl.ds(r0, cr), :])
            la = pad(la_ref[g, pl.ds(r0, cr), :])
            q = qkv[:, :GLA_H * GLA_DK]
            k = qkv[:, GLA_H * GLA_DK:2 * GLA_H * GLA_DK]
            v = qkv[:, 2 * GLA_H * GLA_DK:]
            la3 = _split3(la)
            b = _dot(lmat, la3[0]) + _dot(lmat, la3[1]) + _dot(lmat, la3[2])
            b_last = b[cp - 1:cp, :]
            q_in = q * jnp.exp(b)
            k_in = k * jnp.exp(-b)
            k_end = k * jnp.exp(b_last - b)
            outs = []
            for h in range(GLA_H):
                ks = slice(h * GLA_DK, (h + 1) * GLA_DK)
                vs = slice(h * GLA_DV, (h + 1) * GLA_DV)
                s_prev = state[g, h]
                att = jnp.where(tril, _pdot(_dot_nt, q_in[:, ks], k_in[:, ks], precise), 0.0)
                o = _pdot(_dot, att, v[:, vs], precise) + _pdot(_dot, q_in[:, ks], s_prev, precise)
                dlog = (_dot_tn(la3[0][:, ks], ones) + _dot_tn(la3[1][:, ks], ones)
                        + _dot_tn(la3[2][:, ks], ones))
                state[g, h] = jnp.exp(dlog) * s_prev + _pdot(_dot_tn, k_end[:, ks], v[:, vs], precise)
                ms = jnp.mean(o * o, axis=-1, keepdims=True)
                outs.append(o * lax.rsqrt(ms + LN_EPS) * gnorm[:, vs])
            o_all = jnp.concatenate(outs, axis=1)[:cr] * r_ref[g, pl.ds(r0, cr), :]
            o_ref[g, pl.ds(r0, cr), :] = o_all.astype(o_ref.dtype)
            return carry

        lax.fori_loop(0, nchunk, chunk, 0, unroll=4 if nchunk % 4 == 0 else 1)

    @pl.when(ti == pl.num_programs(1) - 1)
    def _():
        sfin_ref[...] = state[...]


def _gla(gqkv, gla, gr, s0, gnorm, bg, tg, out_dtype, precise):
    if tg % GLA_CHUNK == 0:
        G, cr, cp, tt = 1, GLA_CHUNK, GLA_CHUNK, min(tg, 512)
    else:
        assert tg % SUBLANES == 0 and tg <= 2 * SUBLANES
        G, cr, cp, tt = SUBLANES, tg, 2 * SUBLANES, tg
    assert bg % G == 0 and tg % tt == 0
    blk = lambda n: pl.BlockSpec((G, tt, n), lambda b, t: (b, t, 0))
    sblk = pl.BlockSpec((G, GLA_H, GLA_DK, GLA_DV), lambda b, t: (b, 0, 0, 0))
    kern = functools.partial(_gla_kernel, G=G, cr=cr, cp=cp, nchunk=tt // cr, precise=precise)
    o, sfin = pl.pallas_call(
        kern, grid=(bg // G, tg // tt),
        in_specs=[blk(1024), blk(256), blk(512), sblk, _const_spec(gnorm.shape)],
        out_specs=(blk(512), sblk),
        out_shape=(jax.ShapeDtypeStruct((bg, tg, 512), out_dtype),
                   jax.ShapeDtypeStruct((bg, GLA_H, GLA_DK, GLA_DV), F32)),
        scratch_shapes=[pltpu.VMEM((G, GLA_H, GLA_DK, GLA_DV), F32)],
        compiler_params=_params('parallel', 'arbitrary'), name='gla')(
            gqkv.reshape(bg, tg, 1024), gla.reshape(bg, tg, 256), gr.reshape(bg, tg, 512), s0, gnorm)
    return o.reshape(bg * tg, 512), sfin


def _conv_kernel(bch_ref, buf_ref, w_ref, o_ref, st_ref, prev, *, G, tt):
    @pl.when(pl.program_id(1) == 0)
    def _():
        prev[...] = buf_ref[...]

    w = w_ref[...]
    rowi = lax.broadcasted_iota(jnp.int32, (tt, CONV_D), 0)
    for g in range(G):
        bch = bch_ref[g]
        u = bch[:, CONV_D:2 * CONV_D] * bch[:, 2 * CONV_D:]
        p = prev[g]
        u1 = jnp.where(rowi == 0, p[1:2], pltpu.roll(u, 1, 0))
        u2 = jnp.where(rowi == 0, p[0:1], jnp.where(rowi == 1, p[1:2], pltpu.roll(u, 2, 0)))
        y = w[0:1] * u2 + w[1:2] * u1 + w[2:3] * u
        o_ref[g] = (bch[:, :CONV_D] * y).astype(o_ref.dtype)
        tail = u[tt - (CONV_W - 1):tt]
        prev[g] = tail
        st_ref[g] = tail


def _conv(bch, buf, conv_w, bg, tg, out_dtype):
    if tg >= 512:
        G, tt = 1, 512
    else:
        G, tt = min(bg, 16), tg
    assert bg % G == 0 and tg % tt == 0 and tt >= CONV_W - 1
    sblk = pl.BlockSpec((G, CONV_W - 1, CONV_D), lambda b, t: (b, 0, 0))
    o, st = pl.pallas_call(
        functools.partial(_conv_kernel, G=G, tt=tt), grid=(bg // G, tg // tt),
        in_specs=[pl.BlockSpec((G, tt, 3 * CONV_D), lambda b, t: (b, t, 0)), sblk,
                  _const_spec(conv_w.shape)],
        out_specs=(pl.BlockSpec((G, tt, CONV_D), lambda b, t: (b, t, 0)), sblk),
        out_shape=(jax.ShapeDtypeStruct((bg, tg, CONV_D), out_dtype),
                   jax.ShapeDtypeStruct((bg, CONV_W - 1, CONV_D), F32)),
        scratch_shapes=[pltpu.VMEM((G, CONV_W - 1, CONV_D), F32)],
        compiler_params=_params('parallel', 'arbitrary'), name='conv')(
            bch.reshape(bg, tg, 3 * CONV_D), buf, conv_w)
    return o.reshape(bg * tg, CONV_D), st


def _fox_prompt_kernel(*refs, tq, tk, hb, precise, q_off):
    if precise:
        q_ref, k_ref, v_ref, ql_ref, kl_ref, vl_ref, o_ref, m_scr, acc_scr, sa_scr, sb_scr = refs
    else:
        q_ref, k_ref, v_ref, o_ref, m_scr, acc_scr, sa_scr, sb_scr = refs
    qi = pl.program_id(2) + q_off
    nfull = (qi * tq) // tk
    row = lax.broadcasted_iota(jnp.int32, (tq, tk), 0)
    col = lax.broadcasted_iota(jnp.int32, (tq, tk), 1)
    m_scr[...] = jnp.full_like(m_scr, -jnp.inf)
    acc_scr[...] = jnp.zeros_like(acc_scr)

    def scores(s_ref, t):
        k0 = pl.multiple_of(t * tk, tk)
        for h in range(hb):
            ls = slice(h * AUG, (h + 1) * AUG)
            kt = k_ref[pl.ds(k0, tk), ls]
            s = _dot_nt(q_ref[:, ls], kt)
            if precise:
                s = s + _dot_nt(ql_ref[:, ls], kt) + _dot_nt(q_ref[:, ls], kl_ref[pl.ds(k0, tk), ls])
            s_ref[h] = s

    def update(s_ref, t, masked):
        k0 = pl.multiple_of(t * tk, tk)
        for h in range(hb):
            ls = slice(h * AUG, (h + 1) * AUG)
            s = s_ref[h]
            if masked:
                s = jnp.where(col + k0 <= row + qi * tq, s, -jnp.inf)
            m_prev = m_scr[h]
            m_new = jnp.maximum(m_prev, jnp.max(s, axis=-1, keepdims=True))
            alpha = jnp.exp(m_prev - m_new)
            p = jnp.exp(s - jnp.concatenate([m_new] * (tk // LANES), axis=1))
            pb = p.astype(BF16)
            vt = v_ref[pl.ds(k0, tk), ls]
            pv = _dot(pb, vt)
            if precise:
                pv = pv + _dot((p - pb.astype(F32)).astype(BF16), vt) + _dot(pb, vl_ref[pl.ds(k0, tk), ls])
            acc_scr[h] = alpha * acc_scr[h] + pv
            m_scr[h] = m_new

    npair = nfull // 2
    scores(sa_scr, 0)

    def body(i, carry):
        scores(sb_scr, 2 * i + 1)
        update(sa_scr, 2 * i, False)
        scores(sa_scr, 2 * i + 2)
        update(sb_scr, 2 * i + 1, False)
        return carry

    lax.fori_loop(0, npair, body, 0)

    @pl.when(nfull == 2 * npair)
    def _():
        update(sa_scr, nfull, True)

    @pl.when(nfull != 2 * npair)
    def _():
        scores(sb_scr, nfull)
        update(sa_scr, nfull - 1, False)
        update(sb_scr, nfull, True)
    lane = lax.broadcasted_iota(jnp.int32, (1, AUG), 1)
    outs = []
    for h in range(0, hb, 2):
        o = []
        for hh in (h, h + 1):
            acc = acc_scr[hh]
            o.append(acc / acc[:, FOX_HD:FOX_HD + 1])
        outs.append(jnp.where(lane < FOX_HD, o[0], pltpu.roll(o[1], FOX_HD, 1)))
    o_ref[...] = jnp.concatenate(outs, axis=1).astype(o_ref.dtype)


def _fox_prompt(qkv, bg, tg, n_run=None):
    tq = min(tg, 256)
    tk = min(tg, 512)
    hb = 4
    nq = tg // tq
    precise = len(qkv) == 6
    n_run = nq if n_run is None else n_run
    q_spec = pl.BlockSpec((tq, hb * AUG), lambda b, p, i: (b * nq + (nq - n_run) + i, p))
    kv_spec = pl.BlockSpec((tg, hb * AUG), lambda b, p, i: (b, p), pipeline_mode=pl.Buffered(1))
    specs = [q_spec, kv_spec, kv_spec]
    return pl.pallas_call(
        functools.partial(_fox_prompt_kernel, tq=tq, tk=tk, hb=hb, precise=precise, q_off=nq - n_run),
        grid=(bg, FOX_H // hb, n_run),
        in_specs=specs * (2 if precise else 1),
        out_specs=pl.BlockSpec((tq, hb * FOX_HD), lambda b, p, i: (b * n_run + i, p)),
        out_shape=jax.ShapeDtypeStruct((bg * n_run * tq, BRANCH_W), F32 if precise else BF16),
        scratch_shapes=[pltpu.VMEM((hb, tq, AUG), F32)] * 2 + [pltpu.VMEM((hb, tq, tk), F32)] * 2,
        compiler_params=_params('parallel', 'parallel', 'arbitrary'), name='fox_prompt')(*qkv)


def _fox_sample_kernel(pt_ref, q_ref, kn_ref, vn_ref, lfn_ref, *refs, P, tq):
    del pt_ref
    k_refs, v_refs, lf_refs = refs[:P], refs[P:2 * P], refs[2 * P:3 * P]
    o_ref, qbd, m_scr, l_scr, acc_scr, carry = refs[3 * P:]
    j = pl.program_id(1)
    nrow = tq * FOX_H
    hrow = lax.broadcasted_iota(jnp.int32, (FOX_H, BRANCH_W), 0)
    hcol = lax.broadcasted_iota(jnp.int32, (FOX_H, BRANCH_W), 1)
    hmask = (hcol // FOX_HD) == hrow

    @pl.when(j == 0)
    def _():
        q = q_ref[...]
        qbd[...] = jnp.concatenate(
            [jnp.where(hmask, jnp.broadcast_to(q[t:t + 1, :], (FOX_H, BRANCH_W)), 0.0) for t in range(tq)],
            axis=0).astype(BF16)
        m_scr[...] = jnp.full_like(m_scr, -jnp.inf)
        l_scr[...] = jnp.zeros_like(l_scr)
        acc_scr[...] = jnp.zeros_like(acc_scr)
        carry[...] = jnp.zeros_like(carry)

    def update(qks, lfs, causal, pvs):
        ss = []
        cur = carry[...]
        for qk, (pre, tot) in zip(qks, [_lane_prefix(lf) for lf in lfs]):
            ss.append(qk - jnp.concatenate([cur + pre] * tq, axis=0))
            cur = cur + tot
        carry[...] = cur
        s = jnp.concatenate(ss, axis=1)
        if causal:
            key = lax.broadcasted_iota(jnp.int32, (nrow, PAGE_SIZE), 1)
            tok = lax.broadcasted_iota(jnp.int32, (nrow, PAGE_SIZE), 0) // FOX_H
            s = jnp.where(key <= tok, s, -jnp.inf)
        m_prev = m_scr[...]
        m_new = jnp.maximum(m_prev, jnp.max(s, axis=-1, keepdims=True))
        alpha = jnp.exp(m_prev - m_new)
        p = jnp.exp(s - jnp.concatenate([m_new] * len(qks), axis=1))
        l_scr[...] = alpha * l_scr[...] + jnp.sum(p, axis=-1, keepdims=True)
        pb = p.astype(BF16)
        pv = None
        for i, f in enumerate(pvs):
            term = f(pb[:, i * PAGE_SIZE:(i + 1) * PAGE_SIZE])
            pv = term if pv is None else pv + term
        acc_scr[...] = jnp.concatenate([alpha] * (BRANCH_W // LANES), axis=1) * acc_scr[...] + pv
        m_scr[...] = m_new

    q_bd = qbd[...]
    kts = [k_refs[i][...].reshape(BRANCH_W, PAGE_SIZE).astype(BF16) for i in range(P)]
    vts = [v_refs[i][...].reshape(BRANCH_W, PAGE_SIZE).astype(BF16) for i in range(P)]
    update([_dot(q_bd, kt) for kt in kts], [lf_refs[i][...] for i in range(P)], False,
           [lambda p, vt=vt: _dot_nt(p, vt) for vt in vts])

    @pl.when(j == pl.num_programs(1) - 1)
    def _():
        zpad = jnp.zeros((PAGE_SIZE - tq, BRANCH_W), F32)
        kn = jnp.concatenate([kn_ref[...], zpad], axis=0).astype(BF16)
        vn = jnp.concatenate([vn_ref[...], zpad], axis=0).astype(BF16)
        update([_dot_nt(qbd[...], kn)], [lfn_ref[0]], True, [lambda p: _dot(p, vn)])
        o = acc_scr[...] / jnp.concatenate([l_scr[...]] * (BRANCH_W // LANES), axis=1)
        rows = [jnp.sum(jnp.where(hmask, o[t * FOX_H:(t + 1) * FOX_H], 0.0), axis=0, keepdims=True)
                for t in range(tq)]
        o_ref[...] = jnp.concatenate(rows, axis=0)


def _fox_sample(fq, fk32, fv32, lft, layer, k_cache, v_cache, lf_cache, page_table, db, tq):
    n_pages = page_table.shape[1]
    P = 8 if n_pages % 8 == 0 else (4 if n_pages % 4 == 0 else 1)
    assert tq == SUBLANES
    kt = jnp.transpose(k_cache, (0, 1, 3, 4, 2))
    vt = jnp.transpose(v_cache, (0, 1, 3, 4, 2))
    lfp = jnp.swapaxes(lf_cache, 2, 3)
    lfn = jnp.swapaxes(lft.reshape(FOX_H, db, tq), 0, 1)
    lfn = jnp.pad(lfn, ((0, 0), (0, 0), (0, PAGE_SIZE - tq)))
    tok = lambda: pl.BlockSpec((tq, BRANCH_W), lambda b, j, pt: (b, 0))
    page = lambda i: pl.BlockSpec((None, None, FOX_H, FOX_HD, PAGE_SIZE),
                                  lambda b, j, pt, i=i: (layer, pt[b, j * P + i], 0, 0, 0))
    lfpage = lambda i: pl.BlockSpec((None, None, FOX_H, PAGE_SIZE),
                                    lambda b, j, pt, i=i: (layer, pt[b, j * P + i], 0, 0))
    nrow = tq * FOX_H
    return pl.pallas_call(
        functools.partial(_fox_sample_kernel, P=P, tq=tq),
        grid_spec=pltpu.PrefetchScalarGridSpec(
            num_scalar_prefetch=1, grid=(db, n_pages // P),
            in_specs=[tok(), tok(), tok(), pl.BlockSpec((1, FOX_H, PAGE_SIZE), lambda b, j, pt: (b, 0, 0))]
                     + [page(i) for i in range(P)] + [page(i) for i in range(P)]
                     + [lfpage(i) for i in range(P)],
            out_specs=tok(),
            scratch_shapes=[pltpu.VMEM((nrow, BRANCH_W), BF16), pltpu.VMEM((nrow, LANES), F32),
                            pltpu.VMEM((nrow, LANES), F32), pltpu.VMEM((nrow, BRANCH_W), F32),
                            pltpu.VMEM((FOX_H, LANES), F32)]),
        out_shape=jax.ShapeDtypeStruct((db * tq, BRANCH_W), F32),
        compiler_params=_params('parallel', 'arbitrary'), name='fox_sample')(
            page_table, fq, fk32, fv32, lfn, *([kt] * P), *([vt] * P), *([lfp] * P))


def _merge_kernel(x_ref, og_ref, of_ref, oc_ref, gate_ref, wb_ref, wo_ref, g_ref, b_ref, y_ref, *, precise):
    m = None
    for jb, o_ref in enumerate((og_ref, of_ref, oc_ref)):
        gj = _sigmoid(gate_ref[:, jb * D_MODEL:(jb + 1) * D_MODEL])
        term = gj * _mm(_pieces(o_ref[...].astype(F32), precise), wb_ref.at[:, jb])
        m = term if m is None else m + term
    y = DEEPNORM_ALPHA * x_ref[...] + _mm(_pieces(m, precise), wo_ref)
    y_ref[...] = _layernorm(y, g_ref[...], b_ref[...])


def _merge(x, o_gla, o_fox, o_conv, gate, w, precise):
    rows = x.shape[0]
    tm = _row_tile(rows, 256)
    row = lambda n: pl.BlockSpec((tm, n), lambda i: (i, 0))
    consts = [w['wb'], w['wo'], w['ln1_g'], w['ln1_b']]
    return pl.pallas_call(
        functools.partial(_merge_kernel, precise=precise), grid=(rows // tm,),
        in_specs=[row(D_MODEL), row(BRANCH_W), row(BRANCH_W), row(BRANCH_W), row(N_BRANCH * D_MODEL)]
                 + [_const_spec(a.shape) for a in consts],
        out_specs=row(D_MODEL), out_shape=jax.ShapeDtypeStruct((rows, D_MODEL), F32),
        compiler_params=_params('parallel'), name='merge')(x, o_gla, o_fox, o_conv, gate, *consts)


def _route(probs):
    rows = [probs[i:i + 1, :] for i in range(N_EXPERTS)]
    gscore = []
    for g in range(N_GROUPS):
        r0, r1, r2, r3 = rows[EXP_PER_GROUP * g:EXP_PER_GROUP * (g + 1)]
        a, b = jnp.maximum(r0, r1), jnp.minimum(r0, r1)
        c, d = jnp.maximum(r2, r3), jnp.minimum(r2, r3)
        gscore.append(jnp.maximum(a, c) + jnp.maximum(jnp.minimum(a, c), jnp.maximum(b, d)))
    best, gsel = gscore[0], jnp.zeros_like(gscore[0], dtype=jnp.int32)
    for g in range(1, N_GROUPS):
        upd = gscore[g] > best
        best = jnp.where(upd, gscore[g], best)
        gsel = jnp.where(upd, g, gsel)
    v = []
    for i in range(EXP_PER_GROUP):
        vi = rows[(N_GROUPS - 1) * EXP_PER_GROUP + i]
        for g in range(N_GROUPS - 2, -1, -1):
            vi = jnp.where(gsel == g, rows[g * EXP_PER_GROUP + i], vi)
        v.append(vi)
    v1, i1 = v[0], jnp.zeros_like(gsel)
    for i in range(1, EXP_PER_GROUP):
        upd = v[i] > v1
        v1 = jnp.where(upd, v[i], v1)
        i1 = jnp.where(upd, i, i1)
    v2, i2 = jnp.full_like(v1, -1.0), jnp.full_like(gsel, -1)
    for i in range(EXP_PER_GROUP):
        cand = jnp.where(i1 == i, -1.0, v[i])
        upd = cand > v2
        v2 = jnp.where(upd, cand, v2)
        i2 = jnp.where(upd, i, i2)
    den = v1 + v2
    w1, w2 = v1 / den, v2 / den
    out = []
    for g in range(N_GROUPS):
        for i in range(EXP_PER_GROUP):
            wi = jnp.where(i1 == i, w1, jnp.where(i2 == i, w2, 0.0))
            out.append(jnp.where(gsel == g, wi, 0.0))
    return out


def _moe_kernel(x_ref, wrt_ref, br_ref, wg_ref, wu_ref, wd_ref, g_ref, b_ref, y_ref):
    x = x_ref[...]
    tm = x.shape[0]
    xb = x.astype(BF16)
    x_lo = (x - xb.astype(F32)).astype(BF16)
    wrt = wrt_ref[...]
    w_hi = wrt.astype(BF16)
    w_lo = (wrt - w_hi.astype(F32)).astype(BF16)
    logits = _dot_nt(w_hi, xb) + _dot_nt(w_hi, x_lo) + _dot_nt(w_lo, xb) + br_ref[...]
    e = jnp.exp(logits - jnp.max(logits, axis=0, keepdims=True))
    probs = e / jnp.sum(e, axis=0, keepdims=True)
    gate_rows = _route(probs)
    gate_t = jnp.concatenate(gate_rows + [jnp.zeros((LANES - N_EXPERTS, tm), F32)], axis=0)
    gate = gate_t.T
    acc = jnp.zeros((tm, D_MODEL), F32)
    for ex in range(N_EXPERTS):
        hg = _dot(xb, wg_ref[ex])
        hu = _dot(xb, wu_ref[ex])
        h = (hg * _sigmoid(hg)) * hu * gate[:, ex:ex + 1]
        acc = acc + _dot(h.astype(BF16), wd_ref[ex])
    y = DEEPNORM_ALPHA * x + acc
    y_ref[...] = _layernorm(y, g_ref[...], b_ref[...])


def _moe(x, w):
    rows = x.shape[0]
    tm = _row_tile(rows, 512)
    row = pl.BlockSpec((tm, D_MODEL), lambda i: (i, 0))
    consts = [w['wrt'], w['br'], w['weg'], w['weu'], w['wed'], w['ln2_g'], w['ln2_b']]
    return pl.pallas_call(
        _moe_kernel, grid=(rows // tm,),
        in_specs=[row] + [_const_spec(a.shape) for a in consts],
        out_specs=row, out_shape=jax.ShapeDtypeStruct((rows, D_MODEL), F32),
        compiler_params=_params('parallel'), name='moe')(x, *consts)


def _prep_weights(l, precise, w_in, w_gla_a2, b_gla_a, gla_norm_g, b_fox_f, conv_w, w_branch, w_out, ln1_g, ln1_b,
                  w_router, b_router, w_e_gate, w_e_up, w_e_down, ln2_g, ln2_b):
    wi = w_in[l]
    off = {}
    o = 0
    for name, n in (('gla_q', 256), ('gla_k', 256), ('gla_v', 512), ('gla_a', GLA_LR), ('gla_r', 512),
                    ('fox_q', 512), ('fox_k', 512), ('fox_v', 512), ('fox_f', FOX_H),
                    ('conv', 3 * CONV_D), ('gate', N_BRANCH * D_MODEL)):
        off[name] = (o, o + n)
        o += n
    col = lambda name: wi[:, off[name][0]:off[name][1]]
    padc = lambda a, n: jnp.pad(a, ((0, 0), (0, n - a.shape[1])))
    row2 = lambda a: a.reshape(1, -1)

    def bf(a):
        hi = a.astype(BF16)
        return jnp.stack([hi, (a - hi.astype(F32)).astype(BF16)]) if precise else hi[None]

    ff = col('fox_f')
    aug = lambda a: jnp.pad(a.reshape(D_MODEL, FOX_H, FOX_HD), ((0, 0), (0, 0), (0, AUG - FOX_HD))).reshape(D_MODEL, AUG_W)
    aug_bias = lambda n: jnp.tile(jnp.pad(jnp.ones((n,), F32), (FOX_HD, AUG - FOX_HD - n)), FOX_H).reshape(1, AUG_W)
    return {
        'wqkv': bf(jnp.concatenate([col('gla_q') * (GLA_DK ** -0.5), col('gla_k'), col('gla_v')], axis=1)),
        'wa': bf(padc(col('gla_a'), LANES)),
        'wa2': bf(jnp.pad(w_gla_a2[l], ((0, LANES - GLA_LR), (0, 0)))),
        'ba': row2(b_gla_a[l]),
        'wr': bf(col('gla_r')),
        'wfq': bf((col('fox_q') * (FOX_HD ** -0.5))),
        'wfk': bf(col('fox_k')),
        'wfv': bf(col('fox_v')),
        'wfkt': bf(col('fox_k').T),
        'wfvt': bf(col('fox_v').T),
        'wff': bf(padc(ff, LANES)),
        'bff': padc(row2(b_fox_f[l]), LANES),
        'wfft': bf(jnp.pad(ff.T, ((0, 2 * SUBLANES - FOX_H), (0, 0)))),
        'bfft': b_fox_f[l].reshape(FOX_H, 1),
        'wqa': bf(aug(col('fox_q') * (FOX_HD ** -0.5))),
        'bqa': aug_bias(3),
        'wka': bf(aug(col('fox_k'))),
        'wva': bf(aug(col('fox_v'))),
        'bva': aug_bias(1),
        'wc': bf(col('conv')),
        'wg': bf(col('gate')),
        'gnorm': row2(gla_norm_g[l]),
        'conv_w': conv_w[l],
        'wb': bf(w_branch[l]),
        'wo': bf(w_out[l]),
        'ln1_g': row2(ln1_g[l]), 'ln1_b': row2(ln1_b[l]),
        'wrt': w_router.T, 'br': b_router.reshape(N_EXPERTS, 1),
        'weg': w_e_gate[l].astype(BF16), 'weu': w_e_up[l].astype(BF16), 'wed': w_e_down[l].astype(BF16),
        'ln2_g': row2(ln2_g[l]), 'ln2_b': row2(ln2_b[l]),
    }


TAIL = 256
SPLICE = 128


def _layer(x, w, bg, tg, gla_s0, conv_buf, pools, act_dtype, precise):
    prompt = pools is None
    precise = precise and prompt and tg >= 2 * TAIL
    gqkv, gla, gr, fk, fv, lf, *fox_in = _proj1(x, w, prompt, tg, precise)
    bch, gate = _proj2(x, w, False)
    o_gla, gla_state = _gla(gqkv, gla, gr, gla_s0, w['gnorm'], bg, tg, F32 if precise else act_dtype, precise)
    if prompt:
        o_fox = _fox_prompt(fox_in[:3], bg, tg)
        caches = (fk.reshape(bg, FOX_H, FOX_HD, tg), fv.reshape(bg, FOX_H, FOX_HD, tg), lf)
    else:
        o_fox = _fox_sample(fox_in[0], fk, fv, fox_in[1], *pools, bg, tg)
        caches = (fk.reshape(bg, tg, FOX_H, FOX_HD), fv.reshape(bg, tg, FOX_H, FOX_HD), lf.reshape(bg, tg, FOX_H))
    o_conv, conv_state = _conv(bch, conv_buf, w['conv_w'], bg, tg, act_dtype)
    y = _moe(_merge(x, o_gla, o_fox, o_conv, gate, w, False), w)
    if precise:
        n_t = TAIL // min(tg, 256)
        tail = lambda a: a.reshape(bg, tg, a.shape[-1])[:, tg - TAIL:].reshape(bg * TAIL, a.shape[-1])
        xt = tail(x)
        bch_t, gate_t = _proj2(xt, w, True)
        o_fox_t = _fox_prompt(fox_in, bg, tg, n_run=n_t)
        o_conv_t, _ = _conv(bch_t, jnp.zeros_like(conv_buf), w['conv_w'], bg, TAIL, F32)
        y_t = _moe(_merge(xt, tail(o_gla), o_fox_t, o_conv_t, gate_t, w, True), w)
        y = y.reshape(bg, tg, D_MODEL).at[:, tg - SPLICE:].set(
            y_t.reshape(bg, TAIL, D_MODEL)[:, TAIL - SPLICE:]).reshape(bg * tg, D_MODEL)
    return y, caches + (gla_state, conv_state)


def kernel(x_prompt, x_sample, cache_fox_k, cache_fox_v, cache_fox_logf, state_gla, state_conv, page_table, w_in, w_gla_a2, b_gla_a, gla_norm_g, b_fox_f, conv_w, w_branch, w_out, ln1_g, ln1_b, w_router, b_router, w_e_gate, w_e_up, w_e_down, ln2_g, ln2_b):
    B, T, _ = x_prompt.shape
    DB, TS, _ = x_sample.shape
    xp = x_prompt.reshape(B * T, D_MODEL)
    xs = x_sample.reshape(DB * TS, D_MODEL)
    depth = w_in.shape[0]
    st_p, st_s = [], []
    for l in range(depth):
        precise = l < depth - 1
        w = _prep_weights(l, precise, w_in, w_gla_a2, b_gla_a, gla_norm_g, b_fox_f, conv_w, w_branch, w_out,
                          ln1_g, ln1_b, w_router, b_router, w_e_gate, w_e_up, w_e_down, ln2_g, ln2_b)
        xp, sp = _layer(xp, w, B, T, jnp.zeros((B, GLA_H, GLA_DK, GLA_DV), F32),
                        jnp.zeros((B, CONV_W - 1, CONV_D), F32), None, BF16, precise)
        xs, ss = _layer(xs, w, DB, TS, state_gla[l], state_conv[l],
                        (l, cache_fox_k, cache_fox_v, cache_fox_logf, page_table), F32, False)
        st_p.append(sp)
        st_s.append(ss)
    stack = lambda sts, i: jnp.stack([s[i] for s in sts])
    return (xp.reshape(B, T, D_MODEL), xs.reshape(DB, TS, D_MODEL),
            jnp.transpose(stack(st_p, 0), (0, 1, 4, 2, 3)), jnp.transpose(stack(st_p, 1), (0, 1, 4, 2, 3)),
            jnp.swapaxes(stack(st_p, 2), 2, 3), stack(st_p, 3), stack(st_p, 4),
            stack(st_s, 0), stack(st_s, 1), stack(st_s, 2), stack(st_s, 3), stack(st_s, 4))
```

```python
import functools

import jax
import jax.numpy as jnp
from jax import lax
from jax.experimental import pallas as pl
from jax.experimental.pallas import tpu as pltpu

F32 = jnp.float32
BF16 = jnp.bfloat16

D_MODEL = 1024
DEPTH = 2
PAGE_SIZE = 128
BRANCH_W = D_MODEL // 2
GLA_H = 4
GLA_DK = BRANCH_W // (2 * GLA_H)
GLA_DV = BRANCH_W // GLA_H
GLA_LR = 16
GLA_TAU = 16.0
GLA_CHUNK = 64
FOX_HD = 64
FOX_H = BRANCH_W // FOX_HD
CONV_D = BRANCH_W
CONV_W = 3
N_BRANCH = 3
N_EXPERTS = 16
N_GROUPS = 4
EXP_PER_GROUP = N_EXPERTS // N_GROUPS
D_EXPERT = D_MODEL // 4
LN_EPS = 1e-5
DEEPNORM_ALPHA = (2 * DEPTH) ** 0.25

LANES = 128
SUBLANES = 8
VMEM_LIMIT = 56 * 1024 * 1024

_NT = (((1,), (1,)), ((), ()))
_TN = (((0,), (0,)), ((), ()))


def _dot(a, b):
    return jnp.dot(a, b, preferred_element_type=F32)


def _dot_nt(a, b):
    return lax.dot_general(a, b, _NT, preferred_element_type=F32)


def _dot_tn(a, b):
    return lax.dot_general(a, b, _TN, preferred_element_type=F32)


def _split3(x):
    hi = x.astype(BF16)
    r1 = x - hi.astype(F32)
    mid = r1.astype(BF16)
    lo = (r1 - mid.astype(F32)).astype(BF16)
    return hi, mid, lo


def _pieces(x, precise):
    hi = x.astype(BF16)
    if not precise:
        return (hi,)
    return hi, (x - hi.astype(F32)).astype(BF16)


def _mm(xs, w_ref):
    y = _dot(xs[0], w_ref[0])
    if len(xs) == 2:
        y = y + _dot(xs[1], w_ref[0]) + _dot(xs[0], w_ref[1])
    return y


def _mm_nt(w_ref, xs):
    y = _dot_nt(w_ref[0], xs[0])
    if len(xs) == 2:
        y = y + _dot_nt(w_ref[0], xs[1]) + _dot_nt(w_ref[1], xs[0])
    return y


def _pdot(dot, a, b, precise):
    ap, bp = _pieces(a, precise), _pieces(b, precise)
    y = dot(ap[0], bp[0])
    if precise:
        y = y + dot(ap[1], bp[0]) + dot(ap[0], bp[1])
    return y


def _log_sigmoid(x):
    return jnp.minimum(x, 0.0) - jnp.log1p(jnp.exp(-jnp.abs(x)))


def _sigmoid(x):
    return 1.0 / (1.0 + jnp.exp(-x))


def _layernorm(y, g, b):
    mu = jnp.mean(y, axis=-1, keepdims=True)
    d = y - mu
    var = jnp.mean(d * d, axis=-1, keepdims=True)
    return d * lax.rsqrt(var + LN_EPS) * g + b


def _lane_prefix(x):
    r = lax.broadcasted_iota(jnp.int32, (LANES, 2 * LANES), 0)
    c = lax.broadcasted_iota(jnp.int32, (LANES, 2 * LANES), 1)
    u = jnp.where(r <= c, 1.0, 0.0).astype(BF16)
    hi, mid, lo = _split3(x)
    y = _dot(hi, u) + _dot(mid, u) + _dot(lo, u)
    return y[:, :LANES], y[:, LANES:]


def _const_spec(shape):
    nd = len(shape)
    return pl.BlockSpec(shape, lambda *_: (0,) * nd, pipeline_mode=pl.Buffered(1))


def _params(*sem):
    return pltpu.CompilerParams(dimension_semantics=sem, vmem_limit_bytes=VMEM_LIMIT)


def _row_tile(rows, want):
    t = min(rows, want)
    assert rows % t == 0
    return t


AUG = 2 * FOX_HD
AUG_W = FOX_H * AUG


def _mm_rows(tail, xs, w_ref, store):
    if tail is None or len(xs) == 1:
        store(_mm(xs, w_ref))
        return

    @pl.when(tail)
    def _():
        store(_mm(xs, w_ref))

    @pl.when(jnp.logical_not(tail))
    def _():
        store(_mm(xs[:1], w_ref))


def _gla_cols(xs, tail, wqkv_ref, wa_ref, wa2_ref, ba_ref, wr_ref, gqkv_ref, gla_ref, gr_ref):
    nq = GLA_H * GLA_DK

    def put_q(y):
        gqkv_ref[:, :nq] = y

    def put_r(r):
        gr_ref[...] = r * _sigmoid(r)

    _mm_rows(tail, xs, wqkv_ref.at[:, :, :nq], put_q)
    gqkv_ref[:, nq:] = _mm(xs, wqkv_ref.at[:, :, nq:])
    a_pre = _mm(_pieces(_mm(xs, wa_ref), len(xs) == 2), wa2_ref) + ba_ref[...]
    gla_ref[...] = _log_sigmoid(a_pre) * (1.0 / GLA_TAU)
    _mm_rows(tail, xs, wr_ref, put_r)


def _proj1_prompt_kernel(x_ref, wqkv_ref, wa_ref, wa2_ref, ba_ref, wr_ref, wfk_ref, wfv_ref, wff_ref, bff_ref,
                         wfft_ref, bfft_ref, wqa_ref, bqa_ref, wka_ref, wva_ref, bva_ref,
                         gqkv_ref, gla_ref, gr_ref, fk_ref, fv_ref, lf_ref, qa_ref, ka_ref, va_ref,
                         *rest, tiles_per_seq, tail_tiles, precise):
    tm = x_ref.shape[0]
    lo_refs, carry = rest[:-1], rest[-1]

    @pl.when(pl.program_id(0) % tiles_per_seq == 0)
    def _():
        carry[...] = jnp.zeros_like(carry)

    xs = _pieces(x_ref[...], precise)
    tail = (pl.program_id(0) % tiles_per_seq >= tiles_per_seq - tail_tiles) if precise else None
    _gla_cols(xs, tail, wqkv_ref, wa_ref, wa2_ref, ba_ref, wr_ref, gqkv_ref, gla_ref, gr_ref)
    fk_ref[...] = _mm_nt(wfk_ref, xs[:1])
    fv_ref[...] = _mm_nt(wfv_ref, xs[:1])
    lf_ref[...] = _log_sigmoid(_mm_nt(wfft_ref, xs[:1])[:FOX_H] + bfft_ref[...])
    lf = _log_sigmoid(_mm(xs, wff_ref) + bff_ref[...])
    ri = lax.broadcasted_iota(jnp.int32, (tm, tm), 0)
    ci = lax.broadcasted_iota(jnp.int32, (tm, tm), 1)
    lmat = jnp.where(ci <= ri, 1.0, 0.0).astype(BF16)
    hi, mid, lo = _split3(lf)
    c = _dot(lmat, hi) + _dot(lmat, mid) + _dot(lmat, lo) + carry[0:1, :]
    carry[...] = jnp.broadcast_to(c[tm - 1:tm, :], carry.shape)
    pr = lax.broadcasted_iota(jnp.int32, (LANES, AUG_W), 0)
    pc = lax.broadcasted_iota(jnp.int32, (LANES, AUG_W), 1)
    ka = _mm(xs, wka_ref)
    for j, part in enumerate(_split3(-c)):
        place = jnp.where(pc == pr * AUG + (FOX_HD + j), 1.0, 0.0).astype(BF16)
        ka = ka + _dot(part, place)
    va = _mm(xs, wva_ref) + bva_ref[...]

    def put(i, val, hi_ref):
        hi = val.astype(BF16)
        hi_ref[...] = hi
        if precise:
            lo_refs[i][...] = (val - hi.astype(F32)).astype(BF16)

    put(1, ka, ka_ref)
    put(2, va, va_ref)
    _mm_rows(tail, xs, wqa_ref, lambda qa: put(0, qa + bqa_ref[...], qa_ref))


def _proj1_sample_kernel(x_ref, wqkv_ref, wa_ref, wa2_ref, ba_ref, wr_ref, wfk_ref, wfv_ref, wff_ref, bff_ref,
                         wfft_ref, bfft_ref, wfq_ref,
                         gqkv_ref, gla_ref, gr_ref, fk_ref, fv_ref, lf_ref, fq_ref, lft_ref):
    xs = _pieces(x_ref[...], False)
    _gla_cols(xs, None, wqkv_ref, wa_ref, wa2_ref, ba_ref, wr_ref, gqkv_ref, gla_ref, gr_ref)
    fk_ref[...] = _mm(xs, wfk_ref)
    fv_ref[...] = _mm(xs, wfv_ref)
    lf_ref[...] = _log_sigmoid(_mm(xs, wff_ref) + bff_ref[...])[:, :FOX_H]
    fq_ref[...] = _mm(xs, wfq_ref)
    lft_ref[...] = _log_sigmoid(_mm_nt(wfft_ref, xs)[:FOX_H] + bfft_ref[...])


def _proj1(x, w, prompt, tg, precise):
    rows = x.shape[0]
    tm = _row_tile(rows, 256)
    row = lambda n: pl.BlockSpec((tm, n), lambda i: (i, 0))
    common = ('wqkv', 'wa', 'wa2', 'ba', 'wr', 'wfkt' if prompt else 'wfk', 'wfvt' if prompt else 'wfv',
              'wff', 'bff', 'wfft', 'bfft')
    sds = lambda n, dt: jax.ShapeDtypeStruct((rows, n), dt)
    out_shape = [sds(1024, F32), sds(256, F32), sds(512, F32)]
    out_specs = [row(1024), row(256), row(512)]
    if prompt:
        assert tg % tm == 0
        tps = tg // tm
        names = common + ('wqa', 'bqa', 'wka', 'wva', 'bva')
        kern = functools.partial(_proj1_prompt_kernel, tiles_per_seq=tps, tail_tiles=-(-TAIL // tm),
                                 precise=precise)
        tr = lambda n: pl.BlockSpec((None, n, tm), lambda i: (i // tps, 0, i % tps))
        trs = lambda n: jax.ShapeDtypeStruct((rows // tg, n, tg), F32)
        n_aug = 6 if precise else 3
        out_shape += [trs(512), trs(512), trs(FOX_H)] + [sds(AUG_W, BF16)] * n_aug
        out_specs += [tr(512), tr(512), tr(FOX_H)] + [row(AUG_W)] * n_aug
        scratch = [pltpu.VMEM((SUBLANES, LANES), F32)]
    else:
        names = common + ('wfq',)
        kern = _proj1_sample_kernel
        out_shape += [sds(512, F32), sds(512, F32), sds(FOX_H, F32), sds(512, F32),
                      jax.ShapeDtypeStruct((FOX_H, rows), F32)]
        out_specs += [row(512), row(512), row(FOX_H), row(512), pl.BlockSpec((FOX_H, tm), lambda i: (0, i))]
        scratch = []
    ws = [w[n] for n in names]
    return pl.pallas_call(
        kern, grid=(rows // tm,),
        in_specs=[row(D_MODEL)] + [_const_spec(a.shape) for a in ws],
        out_specs=out_specs, out_shape=out_shape, scratch_shapes=scratch,
        compiler_params=_params('arbitrary'), name='proj1_prompt' if prompt else 'proj1_sample')(x, *ws)


def _proj2_kernel(x_ref, wc_ref, wg_ref, bch_ref, gate_ref, *, precise):
    xs = _pieces(x_ref[...], precise)
    bch_ref[...] = _mm(xs, wc_ref)
    gate_ref[...] = _mm(xs, wg_ref)


def _proj2(x, w, precise):
    rows = x.shape[0]
    tm = _row_tile(rows, 256)
    row = lambda n: pl.BlockSpec((tm, n), lambda i: (i, 0))
    return pl.pallas_call(
        functools.partial(_proj2_kernel, precise=precise), grid=(rows // tm,),
        in_specs=[row(D_MODEL), _const_spec(w['wc'].shape), _const_spec(w['wg'].shape)],
        out_specs=(row(3 * CONV_D), row(N_BRANCH * D_MODEL)),
        out_shape=(jax.ShapeDtypeStruct((rows, 3 * CONV_D), F32),
                   jax.ShapeDtypeStruct((rows, N_BRANCH * D_MODEL), F32)),
        compiler_params=_params('parallel'), name='proj2')(x, w['wc'], w['wg'])


def _gla_kernel(qkv_ref, la_ref, r_ref, s0_ref, g_ref, o_ref, sfin_ref, state, *, G, cr, cp, nchunk, precise):
    ti = pl.program_id(1)

    @pl.when(ti == 0)
    def _():
        state[...] = s0_ref[...]

    ri = lax.broadcasted_iota(jnp.int32, (cp, cp), 0)
    ci = lax.broadcasted_iota(jnp.int32, (cp, cp), 1)
    tril = ci <= ri
    lmat = jnp.where(tril, 1.0, 0.0).astype(BF16)
    ones = jnp.ones((cp, GLA_DV), BF16)
    gnorm = g_ref[...]

    def pad(a):
        if cp == cr:
            return a
        return jnp.concatenate([a, jnp.zeros((cp - cr, a.shape[1]), a.dtype)], axis=0)

    for g in range(G):
        def chunk(c, carry, g=g):
            r0 = pl.multiple_of(c * cr, cr)
            qkv = pad(qkv_ref[g, pl.ds(r0, cr), :])
            la = pad(la_ref[g, pl.ds(r0, cr), :])
            q = qkv[:, :GLA_H * GLA_DK]
            k = qkv[:, GLA_H * GLA_DK:2 * GLA_H * GLA_DK]
            v = qkv[:, 2 * GLA_H * GLA_DK:]
            la3 = _split3(la)
            b = _dot(lmat, la3[0]) + _dot(lmat, la3[1]) + _dot(lmat, la3[2])
            b_last = b[cp - 1:cp, :]
            q_in = q * jnp.exp(b)
            k_in = k * jnp.exp(-b)
            k_end = k * jnp.exp(b_last - b)
            outs = []
            for h in range(GLA_H):
                ks = slice(h * GLA_DK, (h + 1) * GLA_DK)
                vs = slice(h * GLA_DV, (h + 1) * GLA_DV)
                s_prev = state[g, h]
                att = jnp.where(tril, _pdot(_dot_nt, q_in[:, ks], k_in[:, ks], precise), 0.0)
                o = _pdot(_dot, att, v[:, vs], precise) + _pdot(_dot, q_in[:, ks], s_prev, precise)
                dlog = (_dot_tn(la3[0][:, ks], ones) + _dot_tn(la3[1][:, ks], ones)
                        + _dot_tn(la3[2][:, ks], ones))
                state[g, h] = jnp.exp(dlog) * s_prev + _pdot(_dot_tn, k_end[:, ks], v[:, vs], precise)
                ms = jnp.mean(o * o, axis=-1, keepdims=True)
                outs.append(o * lax.rsqrt(ms + LN_EPS) * gnorm[:, vs])
            o_all = jnp.concatenate(outs, axis=1)[:cr] * r_ref[g, pl.ds(r0, cr), :]
            o_ref[g, pl.ds(r0, cr), :] = o_all.astype(o_ref.dtype)
            return carry

        lax.fori_loop(0, nchunk, chunk, 0, unroll=4 if nchunk % 4 == 0 else 1)

    @pl.when(ti == pl.num_programs(1) - 1)
    def _():
        sfin_ref[...] = state[...]


def _gla(gqkv, gla, gr, s0, gnorm, bg, tg, out_dtype, precise):
    if tg % GLA_CHUNK == 0:
        G, cr, cp, tt = 1, GLA_CHUNK, GLA_CHUNK, min(tg, 512)
    else:
        assert tg % SUBLANES == 0 and tg <= 2 * SUBLANES
        G, cr, cp, tt = SUBLANES, tg, 2 * SUBLANES, tg
    assert bg % G == 0 and tg % tt == 0
    blk = lambda n: pl.BlockSpec((G, tt, n), lambda b, t: (b, t, 0))
    sblk = pl.BlockSpec((G, GLA_H, GLA_DK, GLA_DV), lambda b, t: (b, 0, 0, 0))
    kern = functools.partial(_gla_kernel, G=G, cr=cr, cp=cp, nchunk=tt // cr, precise=precise)
    o, sfin = pl.pallas_call(
        kern, grid=(bg // G, tg // tt),
        in_specs=[blk(1024), blk(256), blk(512), sblk, _const_spec(gnorm.shape)],
        out_specs=(blk(512), sblk),
        out_shape=(jax.ShapeDtypeStruct((bg, tg, 512), out_dtype),
                   jax.ShapeDtypeStruct((bg, GLA_H, GLA_DK, GLA_DV), F32)),
        scratch_shapes=[pltpu.VMEM((G, GLA_H, GLA_DK, GLA_DV), F32)],
        compiler_params=_params('parallel', 'arbitrary'), name='gla')(
            gqkv.reshape(bg, tg, 1024), gla.reshape(bg, tg, 256), gr.reshape(bg, tg, 512), s0, gnorm)
    return o.reshape(bg * tg, 512), sfin


def _conv_kernel(bch_ref, buf_ref, w_ref, o_ref, st_ref, prev, *, G, tt):
    @pl.when(pl.program_id(1) == 0)
    def _():
        prev[...] = buf_ref[...]

    w = w_ref[...]
    rowi = lax.broadcasted_iota(jnp.int32, (tt, CONV_D), 0)
    for g in range(G):
        bch = bch_ref[g]
        u = bch[:, CONV_D:2 * CONV_D] * bch[:, 2 * CONV_D:]
        p = prev[g]
        u1 = jnp.where(rowi == 0, p[1:2], pltpu.roll(u, 1, 0))
        u2 = jnp.where(rowi == 0, p[0:1], jnp.where(rowi == 1, p[1:2], pltpu.roll(u, 2, 0)))
        y = w[0:1] * u2 + w[1:2] * u1 + w[2:3] * u
        o_ref[g] = (bch[:, :CONV_D] * y).astype(o_ref.dtype)
        tail = u[tt - (CONV_W - 1):tt]
        prev[g] = tail
        st_ref[g] = tail


def _conv(bch, buf, conv_w, bg, tg, out_dtype):
    if tg >= 512:
        G, tt = 1, 512
    else:
        G, tt = min(bg, 16), tg
    assert bg % G == 0 and tg % tt == 0 and tt >= CONV_W - 1
    sblk = pl.BlockSpec((G, CONV_W - 1, CONV_D), lambda b, t: (b, 0, 0))
    o, st = pl.pallas_call(
        functools.partial(_conv_kernel, G=G, tt=tt), grid=(bg // G, tg // tt),
        in_specs=[pl.BlockSpec((G, tt, 3 * CONV_D), lambda b, t: (b, t, 0)), sblk,
                  _const_spec(conv_w.shape)],
        out_specs=(pl.BlockSpec((G, tt, CONV_D), lambda b, t: (b, t, 0)), sblk),
        out_shape=(jax.ShapeDtypeStruct((bg, tg, CONV_D), out_dtype),
                   jax.ShapeDtypeStruct((bg, CONV_W - 1, CONV_D), F32)),
        scratch_shapes=[pltpu.VMEM((G, CONV_W - 1, CONV_D), F32)],
        compiler_params=_params('parallel', 'arbitrary'), name='conv')(
            bch.reshape(bg, tg, 3 * CONV_D), buf, conv_w)
    return o.reshape(bg * tg, CONV_D), st


def _fox_prompt_kernel(*refs, tq, tk, hb, precise, q_off):
    if precise:
        q_ref, k_ref, v_ref, ql_ref, kl_ref, vl_ref, o_ref, m_scr, acc_scr, sa_scr, sb_scr = refs
    else:
        q_ref, k_ref, v_ref, o_ref, m_scr, acc_scr, sa_scr, sb_scr = refs
    qi = pl.program_id(2) + q_off
    nfull = (qi * tq) // tk
    row = lax.broadcasted_iota(jnp.int32, (tq, tk), 0)
    col = lax.broadcasted_iota(jnp.int32, (tq, tk), 1)
    m_scr[...] = jnp.full_like(m_scr, -jnp.inf)
    acc_scr[...] = jnp.zeros_like(acc_scr)

    def scores(s_ref, t):
        k0 = pl.multiple_of(t * tk, tk)
        for h in range(hb):
            ls = slice(h * AUG, (h + 1) * AUG)
            kt = k_ref[pl.ds(k0, tk), ls]
            s = _dot_nt(q_ref[:, ls], kt)
            if precise:
                s = s + _dot_nt(ql_ref[:, ls], kt) + _dot_nt(q_ref[:, ls], kl_ref[pl.ds(k0, tk), ls])
            s_ref[h] = s

    def update(s_ref, t, masked):
        k0 = pl.multiple_of(t * tk, tk)
        for h in range(hb):
            ls = slice(h * AUG, (h + 1) * AUG)
            s = s_ref[h]
            if masked:
                s = jnp.where(col + k0 <= row + qi * tq, s, -jnp.inf)
            m_prev = m_scr[h]
            m_new = jnp.maximum(m_prev, jnp.max(s, axis=-1, keepdims=True))
            alpha = jnp.exp(m_prev - m_new)
            p = jnp.exp(s - jnp.concatenate([m_new] * (tk // LANES), axis=1))
            pb = p.astype(BF16)
            vt = v_ref[pl.ds(k0, tk), ls]
            pv = _dot(pb, vt)
            if precise:
                pv = pv + _dot((p - pb.astype(F32)).astype(BF16), vt) + _dot(pb, vl_ref[pl.ds(k0, tk), ls])
            acc_scr[h] = alpha * acc_scr[h] + pv
            m_scr[h] = m_new

    npair = nfull // 2
    scores(sa_scr, 0)

    def body(i, carry):
        scores(sb_scr, 2 * i + 1)
        update(sa_scr, 2 * i, False)
        scores(sa_scr, 2 * i + 2)
        update(sb_scr, 2 * i + 1, False)
        return carry

    lax.fori_loop(0, npair, body, 0)

    @pl.when(nfull == 2 * npair)
    def _():
        update(sa_scr, nfull, True)

    @pl.when(nfull != 2 * npair)
    def _():
        scores(sb_scr, nfull)
        update(sa_scr, nfull - 1, False)
        update(sb_scr, nfull, True)
    lane = lax.broadcasted_iota(jnp.int32, (1, AUG), 1)
    outs = []
    for h in range(0, hb, 2):
        o = []
        for hh in (h, h + 1):
            acc = acc_scr[hh]
            o.append(acc / acc[:, FOX_HD:FOX_HD + 1])
        outs.append(jnp.where(lane < FOX_HD, o[0], pltpu.roll(o[1], FOX_HD, 1)))
    o_ref[...] = jnp.concatenate(outs, axis=1).astype(o_ref.dtype)


def _fox_prompt(qkv, bg, tg, n_run=None):
    tq = min(tg, 256)
    tk = min(tg, 512)
    hb = 4
    nq = tg // tq
    precise = len(qkv) == 6
    n_run = nq if n_run is None else n_run
    q_spec = pl.BlockSpec((tq, hb * AUG), lambda b, p, i: (b * nq + (nq - n_run) + i, p))
    kv_spec = pl.BlockSpec((tg, hb * AUG), lambda b, p, i: (b, p), pipeline_mode=pl.Buffered(1))
    specs = [q_spec, kv_spec, kv_spec]
    return pl.pallas_call(
        functools.partial(_fox_prompt_kernel, tq=tq, tk=tk, hb=hb, precise=precise, q_off=nq - n_run),
        grid=(bg, FOX_H // hb, n_run),
        in_specs=specs * (2 if precise else 1),
        out_specs=pl.BlockSpec((tq, hb * FOX_HD), lambda b, p, i: (b * n_run + i, p)),
        out_shape=jax.ShapeDtypeStruct((bg * n_run * tq, BRANCH_W), F32 if precise else BF16),
        scratch_shapes=[pltpu.VMEM((hb, tq, AUG), F32)] * 2 + [pltpu.VMEM((hb, tq, tk), F32)] * 2,
        compiler_params=_params('parallel', 'parallel', 'arbitrary'), name='fox_prompt')(*qkv)


def _fox_sample_kernel(pt_ref, q_ref, kn_ref, vn_ref, lfn_ref, *refs, P, tq, single):
    del pt_ref
    k_refs, v_refs, lf_refs = refs[:P], refs[P:2 * P], refs[2 * P:3 * P]
    o_ref, qbd, m_scr, l_scr, acc_scr, carry = refs[3 * P:]
    j = pl.program_id(1)
    nrow = tq * FOX_H
    hrow = lax.broadcasted_iota(jnp.int32, (FOX_H, BRANCH_W), 0)
    hcol = lax.broadcasted_iota(jnp.int32, (FOX_H, BRANCH_W), 1)
    hmask = (hcol // FOX_HD) == hrow

    @pl.when(j == 0)
    def _():
        q = q_ref[...]
        qbd[...] = jnp.concatenate(
            [jnp.where(hmask, jnp.broadcast_to(q[t:t + 1, :], (FOX_H, BRANCH_W)), 0.0) for t in range(tq)],
            axis=0).astype(BF16)
        m_scr[...] = jnp.full_like(m_scr, -jnp.inf)
        l_scr[...] = jnp.zeros_like(l_scr)
        acc_scr[...] = jnp.zeros_like(acc_scr)
        carry[...] = jnp.zeros_like(carry)

    def update(qks, lfs, own_last, pvs):
        ss = []
        cur = carry[...]
        for i, (qk, (pre, tot)) in enumerate(zip(qks, [_lane_prefix(lf) for lf in lfs])):
            sb = qk - jnp.concatenate([cur + pre] * tq, axis=0)
            if own_last and i == len(qks) - 1:
                key = lax.broadcasted_iota(jnp.int32, (nrow, PAGE_SIZE), 1)
                tok = lax.broadcasted_iota(jnp.int32, (nrow, PAGE_SIZE), 0) // FOX_H
                sb = jnp.where(key <= tok, sb, -jnp.inf)
            ss.append(sb)
            cur = cur + tot
        carry[...] = cur
        s = jnp.concatenate(ss, axis=1)
        m_prev = m_scr[...]
        m_new = jnp.maximum(m_prev, jnp.max(s, axis=-1, keepdims=True))
        alpha = jnp.exp(m_prev - m_new)
        p = jnp.exp(s - jnp.concatenate([m_new] * len(qks), axis=1))
        l_scr[...] = alpha * l_scr[...] + jnp.sum(p, axis=-1, keepdims=True)
        pb = p.astype(BF16)
        pv = None
        for i, f in enumerate(pvs):
            term = f(pb[:, i * PAGE_SIZE:(i + 1) * PAGE_SIZE])
            pv = term if pv is None else pv + term
        acc_scr[...] = jnp.concatenate([alpha] * (BRANCH_W // LANES), axis=1) * acc_scr[...] + pv
        m_scr[...] = m_new

    q_bd = qbd[...]
    kts = [k_refs[i][...].reshape(BRANCH_W, PAGE_SIZE).astype(BF16) for i in range(P)]
    vts = [v_refs[i][...].reshape(BRANCH_W, PAGE_SIZE).astype(BF16) for i in range(P)]
    page_qk = [_dot(q_bd, kt) for kt in kts]
    page_lf = [lf_refs[i][...] for i in range(P)]
    page_pv = [lambda p, vt=vt: _dot_nt(p, vt) for vt in vts]

    def own_block():
        zpad = jnp.zeros((PAGE_SIZE - tq, BRANCH_W), F32)
        kn = jnp.concatenate([kn_ref[...], zpad], axis=0).astype(BF16)
        vn = jnp.concatenate([vn_ref[...], zpad], axis=0).astype(BF16)
        return _dot_nt(q_bd, kn), lfn_ref[0], lambda p: _dot(p, vn)

    def finish():
        o = acc_scr[...] / jnp.concatenate([l_scr[...]] * (BRANCH_W // LANES), axis=1)
        rows = [jnp.sum(jnp.where(hmask, o[t * FOX_H:(t + 1) * FOX_H], 0.0), axis=0, keepdims=True)
                for t in range(tq)]
        o_ref[...] = jnp.concatenate(rows, axis=0)

    if single:
        qk_n, lf_n, pv_n = own_block()
        update(page_qk + [qk_n], page_lf + [lf_n], True, page_pv + [pv_n])
        finish()
    else:
        update(page_qk, page_lf, False, page_pv)

        @pl.when(j == pl.num_programs(1) - 1)
        def _():
            qk_n, lf_n, pv_n = own_block()
            update([qk_n], [lf_n], True, [pv_n])
            finish()


def _fox_sample(fq, fk32, fv32, lft, layer, k_cache, v_cache, lf_cache, page_table, db, tq):
    n_pages = page_table.shape[1]
    P = next(p for p in (16, 8, 4, 2, 1) if n_pages % p == 0)
    assert tq == SUBLANES
    kt = jnp.transpose(k_cache, (0, 1, 3, 4, 2))
    vt = jnp.transpose(v_cache, (0, 1, 3, 4, 2))
    lfp = jnp.swapaxes(lf_cache, 2, 3)
    lfn = jnp.swapaxes(lft.reshape(FOX_H, db, tq), 0, 1)
    lfn = jnp.pad(lfn, ((0, 0), (0, 0), (0, PAGE_SIZE - tq)))
    tok = lambda: pl.BlockSpec((tq, BRANCH_W), lambda b, j, pt: (b, 0))
    page = lambda i: pl.BlockSpec((None, None, FOX_H, FOX_HD, PAGE_SIZE),
                                  lambda b, j, pt, i=i: (layer, pt[b, j * P + i], 0, 0, 0))
    lfpage = lambda i: pl.BlockSpec((None, None, FOX_H, PAGE_SIZE),
                                    lambda b, j, pt, i=i: (layer, pt[b, j * P + i], 0, 0))
    nrow = tq * FOX_H
    return pl.pallas_call(
        functools.partial(_fox_sample_kernel, P=P, tq=tq, single=n_pages == P),
        grid_spec=pltpu.PrefetchScalarGridSpec(
            num_scalar_prefetch=1, grid=(db, n_pages // P),
            in_specs=[tok(), tok(), tok(), pl.BlockSpec((1, FOX_H, PAGE_SIZE), lambda b, j, pt: (b, 0, 0))]
                     + [page(i) for i in range(P)] + [page(i) for i in range(P)]
                     + [lfpage(i) for i in range(P)],
            out_specs=tok(),
            scratch_shapes=[pltpu.VMEM((nrow, BRANCH_W), BF16), pltpu.VMEM((nrow, LANES), F32),
                            pltpu.VMEM((nrow, LANES), F32), pltpu.VMEM((nrow, BRANCH_W), F32),
                            pltpu.VMEM((FOX_H, LANES), F32)]),
        out_shape=jax.ShapeDtypeStruct((db * tq, BRANCH_W), F32),
        compiler_params=_params('parallel', 'arbitrary'), name='fox_sample')(
            page_table, fq, fk32, fv32, lfn, *([kt] * P), *([vt] * P), *([lfp] * P))


def _merge_kernel(x_ref, og_ref, of_ref, oc_ref, gate_ref, wb_ref, wo_ref, g_ref, b_ref, y_ref, *, precise):
    m = None
    for jb, o_ref in enumerate((og_ref, of_ref, oc_ref)):
        gj = _sigmoid(gate_ref[:, jb * D_MODEL:(jb + 1) * D_MODEL])
        term = gj * _mm(_pieces(o_ref[...].astype(F32), precise), wb_ref.at[:, jb])
        m = term if m is None else m + term
    y = DEEPNORM_ALPHA * x_ref[...] + _mm(_pieces(m, precise), wo_ref)
    y_ref[...] = _layernorm(y, g_ref[...], b_ref[...])


def _merge(x, o_gla, o_fox, o_conv, gate, w, precise):
    rows = x.shape[0]
    tm = _row_tile(rows, 256)
    row = lambda n: pl.BlockSpec((tm, n), lambda i: (i, 0))
    consts = [w['wb'], w['wo'], w['ln1_g'], w['ln1_b']]
    return pl.pallas_call(
        functools.partial(_merge_kernel, precise=precise), grid=(rows // tm,),
        in_specs=[row(D_MODEL), row(BRANCH_W), row(BRANCH_W), row(BRANCH_W), row(N_BRANCH * D_MODEL)]
                 + [_const_spec(a.shape) for a in consts],
        out_specs=row(D_MODEL), out_shape=jax.ShapeDtypeStruct((rows, D_MODEL), F32),
        compiler_params=_params('parallel'), name='merge')(x, o_gla, o_fox, o_conv, gate, *consts)


def _route(probs):
    rows = [probs[i:i + 1, :] for i in range(N_EXPERTS)]
    gscore = []
    for g in range(N_GROUPS):
        r0, r1, r2, r3 = rows[EXP_PER_GROUP * g:EXP_PER_GROUP * (g + 1)]
        a, b = jnp.maximum(r0, r1), jnp.minimum(r0, r1)
        c, d = jnp.maximum(r2, r3), jnp.minimum(r2, r3)
        gscore.append(jnp.maximum(a, c) + jnp.maximum(jnp.minimum(a, c), jnp.maximum(b, d)))
    best, gsel = gscore[0], jnp.zeros_like(gscore[0], dtype=jnp.int32)
    for g in range(1, N_GROUPS):
        upd = gscore[g] > best
        best = jnp.where(upd, gscore[g], best)
        gsel = jnp.where(upd, g, gsel)
    v = []
    for i in range(EXP_PER_GROUP):
        vi = rows[(N_GROUPS - 1) * EXP_PER_GROUP + i]
        for g in range(N_GROUPS - 2, -1, -1):
            vi = jnp.where(gsel == g, rows[g * EXP_PER_GROUP + i], vi)
        v.append(vi)
    v1, i1 = v[0], jnp.zeros_like(gsel)
    for i in range(1, EXP_PER_GROUP):
        upd = v[i] > v1
        v1 = jnp.where(upd, v[i], v1)
        i1 = jnp.where(upd, i, i1)
    v2, i2 = jnp.full_like(v1, -1.0), jnp.full_like(gsel, -1)
    for i in range(EXP_PER_GROUP):
        cand = jnp.where(i1 == i, -1.0, v[i])
        upd = cand > v2
        v2 = jnp.where(upd, cand, v2)
        i2 = jnp.where(upd, i, i2)
    den = v1 + v2
    w1, w2 = v1 / den, v2 / den
    out = []
    for g in range(N_GROUPS):
        for i in range(EXP_PER_GROUP):
            wi = jnp.where(i1 == i, w1, jnp.where(i2 == i, w2, 0.0))
            out.append(jnp.where(gsel == g, wi, 0.0))
    return out


def _moe_kernel(x_ref, wrt_ref, br_ref, wg_ref, wu_ref, wd_ref, g_ref, b_ref, y_ref):
    x = x_ref[...]
    tm = x.shape[0]
    xb = x.astype(BF16)
    x_lo = (x - xb.astype(F32)).astype(BF16)
    wrt = wrt_ref[...]
    w_hi = wrt.astype(BF16)
    w_lo = (wrt - w_hi.astype(F32)).astype(BF16)
    logits = _dot_nt(w_hi, xb) + _dot_nt(w_hi, x_lo) + _dot_nt(w_lo, xb) + br_ref[...]
    e = jnp.exp(logits - jnp.max(logits, axis=0, keepdims=True))
    probs = e / jnp.sum(e, axis=0, keepdims=True)
    gate_rows = _route(probs)
    gate_t = jnp.concatenate(gate_rows + [jnp.zeros((LANES - N_EXPERTS, tm), F32)], axis=0)
    gate = gate_t.T
    acc = jnp.zeros((tm, D_MODEL), F32)
    for ex in range(N_EXPERTS):
        hg = _dot(xb, wg_ref[ex])
        hu = _dot(xb, wu_ref[ex])
        h = (hg * _sigmoid(hg)) * hu * gate[:, ex:ex + 1]
        acc = acc + _dot(h.astype(BF16), wd_ref[ex])
    y = DEEPNORM_ALPHA * x + acc
    y_ref[...] = _layernorm(y, g_ref[...], b_ref[...])


def _moe(x, w):
    rows = x.shape[0]
    tm = _row_tile(rows, 512)
    row = pl.BlockSpec((tm, D_MODEL), lambda i: (i, 0))
    consts = [w['wrt'], w['br'], w['weg'], w['weu'], w['wed'], w['ln2_g'], w['ln2_b']]
    return pl.pallas_call(
        _moe_kernel, grid=(rows // tm,),
        in_specs=[row] + [_const_spec(a.shape) for a in consts],
        out_specs=row, out_shape=jax.ShapeDtypeStruct((rows, D_MODEL), F32),
        compiler_params=_params('parallel'), name='moe')(x, *consts)


def _prep_weights(l, precise, w_in, w_gla_a2, b_gla_a, gla_norm_g, b_fox_f, conv_w, w_branch, w_out, ln1_g, ln1_b,
                  w_router, b_router, w_e_gate, w_e_up, w_e_down, ln2_g, ln2_b):
    wi = w_in[l]
    off = {}
    o = 0
    for name, n in (('gla_q', 256), ('gla_k', 256), ('gla_v', 512), ('gla_a', GLA_LR), ('gla_r', 512),
                    ('fox_q', 512), ('fox_k', 512), ('fox_v', 512), ('fox_f', FOX_H),
                    ('conv', 3 * CONV_D), ('gate', N_BRANCH * D_MODEL)):
        off[name] = (o, o + n)
        o += n
    col = lambda name: wi[:, off[name][0]:off[name][1]]
    padc = lambda a, n: jnp.pad(a, ((0, 0), (0, n - a.shape[1])))
    row2 = lambda a: a.reshape(1, -1)

    def bf(a):
        hi = a.astype(BF16)
        return jnp.stack([hi, (a - hi.astype(F32)).astype(BF16)]) if precise else hi[None]

    ff = col('fox_f')
    aug = lambda a: jnp.pad(a.reshape(D_MODEL, FOX_H, FOX_HD), ((0, 0), (0, 0), (0, AUG - FOX_HD))).reshape(D_MODEL, AUG_W)
    aug_bias = lambda n: jnp.tile(jnp.pad(jnp.ones((n,), F32), (FOX_HD, AUG - FOX_HD - n)), FOX_H).reshape(1, AUG_W)
    return {
        'wqkv': bf(jnp.concatenate([col('gla_q') * (GLA_DK ** -0.5), col('gla_k'), col('gla_v')], axis=1)),
        'wa': bf(padc(col('gla_a'), LANES)),
        'wa2': bf(jnp.pad(w_gla_a2[l], ((0, LANES - GLA_LR), (0, 0)))),
        'ba': row2(b_gla_a[l]),
        'wr': bf(col('gla_r')),
        'wfq': bf((col('fox_q') * (FOX_HD ** -0.5))),
        'wfk': bf(col('fox_k')),
        'wfv': bf(col('fox_v')),
        'wfkt': bf(col('fox_k').T),
        'wfvt': bf(col('fox_v').T),
        'wff': bf(padc(ff, LANES)),
        'bff': padc(row2(b_fox_f[l]), LANES),
        'wfft': bf(jnp.pad(ff.T, ((0, 2 * SUBLANES - FOX_H), (0, 0)))),
        'bfft': b_fox_f[l].reshape(FOX_H, 1),
        'wqa': bf(aug(col('fox_q') * (FOX_HD ** -0.5))),
        'bqa': aug_bias(3),
        'wka': bf(aug(col('fox_k'))),
        'wva': bf(aug(col('fox_v'))),
        'bva': aug_bias(1),
        'wc': bf(col('conv')),
        'wg': bf(col('gate')),
        'gnorm': row2(gla_norm_g[l]),
        'conv_w': conv_w[l],
        'wb': bf(w_branch[l]),
        'wo': bf(w_out[l]),
        'ln1_g': row2(ln1_g[l]), 'ln1_b': row2(ln1_b[l]),
        'wrt': w_router.T, 'br': b_router.reshape(N_EXPERTS, 1),
        'weg': w_e_gate[l].astype(BF16), 'weu': w_e_up[l].astype(BF16), 'wed': w_e_down[l].astype(BF16),
        'ln2_g': row2(ln2_g[l]), 'ln2_b': row2(ln2_b[l]),
    }


TAIL = 256
SPLICE = 128


def _layer(x, w, bg, tg, gla_s0, conv_buf, pools, act_dtype, precise):
    prompt = pools is None
    precise = precise and prompt and tg >= 2 * TAIL
    gqkv, gla, gr, fk, fv, lf, *fox_in = _proj1(x, w, prompt, tg, precise)
    bch, gate = _proj2(x, w, False)
    o_gla, gla_state = _gla(gqkv, gla, gr, gla_s0, w['gnorm'], bg, tg, F32 if precise else act_dtype, precise)
    if prompt:
        o_fox = _fox_prompt(fox_in[:3], bg, tg)
        caches = (fk.reshape(bg, FOX_H, FOX_HD, tg), fv.reshape(bg, FOX_H, FOX_HD, tg), lf)
    else:
        o_fox = _fox_sample(fox_in[0], fk, fv, fox_in[1], *pools, bg, tg)
        caches = (fk.reshape(bg, tg, FOX_H, FOX_HD), fv.reshape(bg, tg, FOX_H, FOX_HD), lf.reshape(bg, tg, FOX_H))
    o_conv, conv_state = _conv(bch, conv_buf, w['conv_w'], bg, tg, act_dtype)
    y = _moe(_merge(x, o_gla, o_fox, o_conv, gate, w, False), w)
    if precise:
        n_t = TAIL // min(tg, 256)
        tail = lambda a: a.reshape(bg, tg, a.shape[-1])[:, tg - TAIL:].reshape(bg * TAIL, a.shape[-1])
        xt = tail(x)
        bch_t, gate_t = _proj2(xt, w, True)
        o_fox_t = _fox_prompt(fox_in, bg, tg, n_run=n_t)
        o_conv_t, _ = _conv(bch_t, jnp.zeros_like(conv_buf), w['conv_w'], bg, TAIL, F32)
        y_t = _moe(_merge(xt, tail(o_gla), o_fox_t, o_conv_t, gate_t, w, True), w)
        y = y.reshape(bg, tg, D_MODEL).at[:, tg - SPLICE:].set(
            y_t.reshape(bg, TAIL, D_MODEL)[:, TAIL - SPLICE:]).reshape(bg * tg, D_MODEL)
    return y, caches + (gla_state, conv_state)


def kernel(x_prompt, x_sample, cache_fox_k, cache_fox_v, cache_fox_logf, state_gla, state_conv, page_table, w_in, w_gla_a2, b_gla_a, gla_norm_g, b_fox_f, conv_w, w_branch, w_out, ln1_g, ln1_b, w_router, b_router, w_e_gate, w_e_up, w_e_down, ln2_g, ln2_b):
    B, T, _ = x_prompt.shape
    DB, TS, _ = x_sample.shape
    xp = x_prompt.reshape(B * T, D_MODEL)
    xs = x_sample.reshape(DB * TS, D_MODEL)
    depth = w_in.shape[0]
    st_p, st_s = [], []
    for l in range(depth):
        precise = l < depth - 1
        w = _prep_weights(l, precise, w_in, w_gla_a2, b_gla_a, gla_norm_g, b_fox_f, conv_w, w_branch, w_out,
                          ln1_g, ln1_b, w_router, b_router, w_e_gate, w_e_up, w_e_down, ln2_g, ln2_b)
        xp, sp = _layer(xp, w, B, T, jnp.zeros((B, GLA_H, GLA_DK, GLA_DV), F32),
                        jnp.zeros((B, CONV_W - 1, CONV_D), F32), None, BF16, precise)
        xs, ss = _layer(xs, w, DB, TS, state_gla[l], state_conv[l],
                        (l, cache_fox_k, cache_fox_v, cache_fox_logf, page_table), F32, False)
        st_p.append(sp)
        st_s.append(ss)
    stack = lambda sts, i: jnp.stack([s[i] for s in sts])
    return (xp.reshape(B, T, D_MODEL), xs.reshape(DB, TS, D_MODEL),
            jnp.transpose(stack(st_p, 0), (0, 1, 4, 2, 3)), jnp.transpose(stack(st_p, 1), (0, 1, 4, 2, 3)),
            jnp.swapaxes(stack(st_p, 2), 2, 3), stack(st_p, 3), stack(st_p, 4),
            stack(st_s, 0), stack(st_s, 1), stack(st_s, 2), stack(st_s, 3), stack(st_s, 4))
```

```python
import functools

import jax
import jax.numpy as jnp
from jax import lax
from jax.experimental import pallas as pl
from jax.experimental.pallas import tpu as pltpu

F32 = jnp.float32
BF16 = jnp.bfloat16

D_MODEL = 1024
DEPTH = 2
PAGE_SIZE = 128
BRANCH_W = D_MODEL // 2
GLA_H = 4
GLA_DK = BRANCH_W // (2 * GLA_H)
GLA_DV = BRANCH_W // GLA_H
GLA_LR = 16
GLA_TAU = 16.0
GLA_CHUNK = 64
FOX_HD = 64
FOX_H = BRANCH_W // FOX_HD
CONV_D = BRANCH_W
CONV_W = 3
N_BRANCH = 3
N_EXPERTS = 16
N_GROUPS = 4
EXP_PER_GROUP = N_EXPERTS // N_GROUPS
D_EXPERT = D_MODEL // 4
LN_EPS = 1e-5
DEEPNORM_ALPHA = (2 * DEPTH) ** 0.25

LANES = 128
SUBLANES = 8
VMEM_LIMIT = 56 * 1024 * 1024

_NT = (((1,), (1,)), ((), ()))
_TN = (((0,), (0,)), ((), ()))


def _dot(a, b):
    return jnp.dot(a, b, preferred_element_type=F32)


def _dot_nt(a, b):
    return lax.dot_general(a, b, _NT, preferred_element_type=F32)


def _dot_tn(a, b):
    return lax.dot_general(a, b, _TN, preferred_element_type=F32)


def _split3(x):
    hi = x.astype(BF16)
    r1 = x - hi.astype(F32)
    mid = r1.astype(BF16)
    lo = (r1 - mid.astype(F32)).astype(BF16)
    return hi, mid, lo


def _pieces(x, precise):
    hi = x.astype(BF16)
    if not precise:
        return (hi,)
    return hi, (x - hi.astype(F32)).astype(BF16)


def _mm(xs, w_ref):
    y = _dot(xs[0], w_ref[0])
    if len(xs) == 2:
        y = y + _dot(xs[1], w_ref[0]) + _dot(xs[0], w_ref[1])
    return y


def _mm_nt(w_ref, xs):
    y = _dot_nt(w_ref[0], xs[0])
    if len(xs) == 2:
        y = y + _dot_nt(w_ref[0], xs[1]) + _dot_nt(w_ref[1], xs[0])
    return y


def _pdot(dot, a, b, precise):
    ap, bp = _pieces(a, precise), _pieces(b, precise)
    y = dot(ap[0], bp[0])
    if precise:
        y = y + dot(ap[1], bp[0]) + dot(ap[0], bp[1])
    return y


def _log_sigmoid(x):
    return jnp.minimum(x, 0.0) - jnp.log1p(jnp.exp(-jnp.abs(x)))


def _sigmoid(x):
    return 1.0 / (1.0 + jnp.exp(-x))


def _layernorm(y, g, b):
    mu = jnp.mean(y, axis=-1, keepdims=True)
    d = y - mu
    var = jnp.mean(d * d, axis=-1, keepdims=True)
    return d * lax.rsqrt(var + LN_EPS) * g + b


def _lane_prefix(x):
    r = lax.broadcasted_iota(jnp.int32, (LANES, 2 * LANES), 0)
    c = lax.broadcasted_iota(jnp.int32, (LANES, 2 * LANES), 1)
    u = jnp.where(r <= c, 1.0, 0.0).astype(BF16)
    hi, mid, lo = _split3(x)
    y = _dot(hi, u) + _dot(mid, u) + _dot(lo, u)
    return y[:, :LANES], y[:, LANES:]


def _const_spec(shape):
    nd = len(shape)
    return pl.BlockSpec(shape, lambda *_: (0,) * nd, pipeline_mode=pl.Buffered(1))


def _params(*sem):
    return pltpu.CompilerParams(dimension_semantics=sem, vmem_limit_bytes=VMEM_LIMIT)


def _row_tile(rows, want):
    t = min(rows, want)
    assert rows % t == 0
    return t


AUG = 2 * FOX_HD
AUG_W = FOX_H * AUG


def _mm_rows(tail, xs, w_ref, store):
    if tail is None or len(xs) == 1:
        store(_mm(xs, w_ref))
        return

    @pl.when(tail)
    def _():
        store(_mm(xs, w_ref))

    @pl.when(jnp.logical_not(tail))
    def _():
        store(_mm(xs[:1], w_ref))


def _gla_cols(xs, tail, wqkv_ref, wa_ref, wa2_ref, ba_ref, wr_ref, gqkv_ref, gla_ref, gr_ref):
    nq = GLA_H * GLA_DK

    def put_q(y):
        gqkv_ref[:, :nq] = y

    def put_r(r):
        gr_ref[...] = r * _sigmoid(r)

    _mm_rows(tail, xs, wqkv_ref.at[:, :, :nq], put_q)
    gqkv_ref[:, nq:] = _mm(xs, wqkv_ref.at[:, :, nq:])
    a_pre = _mm(_pieces(_mm(xs, wa_ref), len(xs) == 2), wa2_ref) + ba_ref[...]
    gla_ref[...] = _log_sigmoid(a_pre) * (1.0 / GLA_TAU)
    _mm_rows(tail, xs, wr_ref, put_r)


def _proj1_prompt_kernel(x_ref, wqkv_ref, wa_ref, wa2_ref, ba_ref, wr_ref, wfk_ref, wfv_ref, wff_ref, bff_ref,
                         wfft_ref, bfft_ref, wqa_ref, wka_ref, wva_ref, pad_ref,
                         gqkv_ref, gla_ref, gr_ref, fk_ref, fv_ref, lf_ref, qa_ref, ka_ref, va_ref,
                         *rest, tiles_per_seq, tail_tiles, precise):
    tm = x_ref.shape[0]
    lo_refs, carry = rest[:-1], rest[-1]

    @pl.when(pl.program_id(0) % tiles_per_seq == 0)
    def _():
        carry[...] = jnp.zeros_like(carry)

    xs = _pieces(x_ref[...], precise)
    tail = (pl.program_id(0) % tiles_per_seq >= tiles_per_seq - tail_tiles) if precise else None
    _gla_cols(xs, tail, wqkv_ref, wa_ref, wa2_ref, ba_ref, wr_ref, gqkv_ref, gla_ref, gr_ref)
    fk_ref[...] = _mm_nt(wfk_ref, xs[:1])
    fv_ref[...] = _mm_nt(wfv_ref, xs[:1])
    lf_ref[...] = _log_sigmoid(_mm_nt(wfft_ref, xs[:1])[:FOX_H] + bfft_ref[...])
    lf = _log_sigmoid(_mm(xs, wff_ref) + bff_ref[...])
    ri = lax.broadcasted_iota(jnp.int32, (tm, tm), 0)
    ci = lax.broadcasted_iota(jnp.int32, (tm, tm), 1)
    lmat = jnp.where(ci <= ri, 1.0, 0.0).astype(BF16)
    hi, mid, lo = _split3(lf)
    c = _dot(lmat, hi) + _dot(lmat, mid) + _dot(lmat, lo) + carry[0:1, :]
    carry[...] = jnp.broadcast_to(c[tm - 1:tm, :], carry.shape)
    pr = lax.broadcasted_iota(jnp.int32, (LANES, BRANCH_W), 0)
    pc = lax.broadcasted_iota(jnp.int32, (LANES, BRANCH_W), 1)
    kpad = None
    for j, part in enumerate(_split3(-c)):
        term = _dot(part, jnp.where(pc == pr * FOX_HD + j, 1.0, 0.0).astype(BF16))
        kpad = term if kpad is None else kpad + term
    ones = lambda n: jnp.broadcast_to(pad_ref[n - 1:n, :], (tm, BRANCH_W))

    def put(i, val, pad, hi_ref):
        parts = []
        for h in range(FOX_H):
            sl = slice(h * FOX_HD, (h + 1) * FOX_HD)
            parts += [val[:, sl], pad[:, sl]]
        aug = jnp.concatenate(parts, axis=1)
        hi = aug.astype(BF16)
        hi_ref[...] = hi
        if precise:
            lo_refs[i][...] = (aug - hi.astype(F32)).astype(BF16)

    put(1, _mm(xs, wka_ref), kpad, ka_ref)
    put(2, _mm(xs, wva_ref), ones(1), va_ref)
    _mm_rows(tail, xs, wqa_ref, lambda q: put(0, q, ones(3), qa_ref))


def _proj1_sample_kernel(x_ref, wqkv_ref, wa_ref, wa2_ref, ba_ref, wr_ref, wfk_ref, wfv_ref, wff_ref, bff_ref,
                         wfft_ref, bfft_ref, wfq_ref,
                         gqkv_ref, gla_ref, gr_ref, fk_ref, fv_ref, lf_ref, fq_ref, lft_ref):
    xs = _pieces(x_ref[...], False)
    _gla_cols(xs, None, wqkv_ref, wa_ref, wa2_ref, ba_ref, wr_ref, gqkv_ref, gla_ref, gr_ref)
    fk_ref[...] = _mm(xs, wfk_ref)
    fv_ref[...] = _mm(xs, wfv_ref)
    lf_ref[...] = _log_sigmoid(_mm(xs, wff_ref) + bff_ref[...])[:, :FOX_H]
    fq_ref[...] = _mm(xs, wfq_ref)
    lft_ref[...] = _log_sigmoid(_mm_nt(wfft_ref, xs)[:FOX_H] + bfft_ref[...])


def _proj1(x, w, prompt, tg, precise):
    rows = x.shape[0]
    tm = _row_tile(rows, 256)
    row = lambda n: pl.BlockSpec((tm, n), lambda i: (i, 0))
    common = ('wqkv', 'wa', 'wa2', 'ba', 'wr', 'wfkt' if prompt else 'wfk', 'wfvt' if prompt else 'wfv',
              'wff', 'bff', 'wfft', 'bfft')
    sds = lambda n, dt: jax.ShapeDtypeStruct((rows, n), dt)
    out_shape = [sds(1024, F32), sds(256, F32), sds(512, F32)]
    out_specs = [row(1024), row(256), row(512)]
    if prompt:
        assert tg % tm == 0
        tps = tg // tm
        names = common + ('wfq', 'wfk', 'wfv', 'pad_ones')
        kern = functools.partial(_proj1_prompt_kernel, tiles_per_seq=tps, tail_tiles=-(-TAIL // tm),
                                 precise=precise)
        tr = lambda n: pl.BlockSpec((None, n, tm), lambda i: (i // tps, 0, i % tps))
        trs = lambda n: jax.ShapeDtypeStruct((rows // tg, n, tg), F32)
        n_aug = 6 if precise else 3
        out_shape += [trs(512), trs(512), trs(FOX_H)] + [sds(AUG_W, BF16)] * n_aug
        out_specs += [tr(512), tr(512), tr(FOX_H)] + [row(AUG_W)] * n_aug
        scratch = [pltpu.VMEM((SUBLANES, LANES), F32)]
    else:
        names = common + ('wfq',)
        kern = _proj1_sample_kernel
        out_shape += [sds(512, F32), sds(512, F32), sds(FOX_H, F32), sds(512, F32),
                      jax.ShapeDtypeStruct((FOX_H, rows), F32)]
        out_specs += [row(512), row(512), row(FOX_H), row(512), pl.BlockSpec((FOX_H, tm), lambda i: (0, i))]
        scratch = []
    ws = [w[n] for n in names]
    return pl.pallas_call(
        kern, grid=(rows // tm,),
        in_specs=[row(D_MODEL)] + [_const_spec(a.shape) for a in ws],
        out_specs=out_specs, out_shape=out_shape, scratch_shapes=scratch,
        compiler_params=_params('arbitrary'), name='proj1_prompt' if prompt else 'proj1_sample')(x, *ws)


def _proj2_kernel(x_ref, wc_ref, wg_ref, bch_ref, gate_ref, *, precise):
    xs = _pieces(x_ref[...], precise)
    bch_ref[...] = _mm(xs, wc_ref)
    gate_ref[...] = _mm(xs, wg_ref)


def _proj2(x, w, precise):
    rows = x.shape[0]
    tm = _row_tile(rows, 256)
    row = lambda n: pl.BlockSpec((tm, n), lambda i: (i, 0))
    return pl.pallas_call(
        functools.partial(_proj2_kernel, precise=precise), grid=(rows // tm,),
        in_specs=[row(D_MODEL), _const_spec(w['wc'].shape), _const_spec(w['wg'].shape)],
        out_specs=(row(3 * CONV_D), row(N_BRANCH * D_MODEL)),
        out_shape=(jax.ShapeDtypeStruct((rows, 3 * CONV_D), F32),
                   jax.ShapeDtypeStruct((rows, N_BRANCH * D_MODEL), F32)),
        compiler_params=_params('parallel'), name='proj2')(x, w['wc'], w['wg'])


def _gla_kernel(qkv_ref, la_ref, r_ref, s0_ref, g_ref, o_ref, sfin_ref, state, *, G, cr, cp, nchunk, precise):
    ti = pl.program_id(1)

    @pl.when(ti == 0)
    def _():
        state[...] = s0_ref[...]

    ri = lax.broadcasted_iota(jnp.int32, (cp, cp), 0)
    ci = lax.broadcasted_iota(jnp.int32, (cp, cp), 1)
    tril = ci <= ri
    lmat = jnp.where(tril, 1.0, 0.0).astype(BF16)
    ones = jnp.ones((cp, GLA_DV), BF16)
    gnorm = g_ref[...]

    def pad(a):
        if cp == cr:
            return a
        return jnp.concatenate([a, jnp.zeros((cp - cr, a.shape[1]), a.dtype)], axis=0)

    def chunk(c, carry):
        r0 = pl.multiple_of(c * cr, cr)
        for g in range(G):
            qkv = pad(qkv_ref[g, pl.ds(r0, cr), :])
            la = pad(la_ref[g, pl.ds(r0, cr), :])
            q = qkv[:, :GLA_H * GLA_DK]
            k = qkv[:, GLA_H * GLA_DK:2 * GLA_H * GLA_DK]
            v = qkv[:, 2 * GLA_H * GLA_DK:]
            la3 = _split3(la)
            b = _dot(lmat, la3[0]) + _dot(lmat, la3[1]) + _dot(lmat, la3[2])
            b_last = b[cp - 1:cp, :]
            q_in = q * jnp.exp(b)
            k_in = k * jnp.exp(-b)
            k_end = k * jnp.exp(b_last - b)
            outs = []
            for h in range(GLA_H):
                ks = slice(h * GLA_DK, (h + 1) * GLA_DK)
                vs = slice(h * GLA_DV, (h + 1) * GLA_DV)
                s_prev = state[g, h]
                att = jnp.where(tril, _pdot(_dot_nt, q_in[:, ks], k_in[:, ks], precise), 0.0)
                o = _pdot(_dot, att, v[:, vs], precise) + _pdot(_dot, q_in[:, ks], s_prev, precise)
                dlog = (_dot_tn(la3[0][:, ks], ones) + _dot_tn(la3[1][:, ks], ones)
                        + _dot_tn(la3[2][:, ks], ones))
                state[g, h] = jnp.exp(dlog) * s_prev + _pdot(_dot_tn, k_end[:, ks], v[:, vs], precise)
                ms = jnp.mean(o * o, axis=-1, keepdims=True)
                outs.append(o * lax.rsqrt(ms + LN_EPS) * gnorm[:, vs])
            o_all = jnp.concatenate(outs, axis=1)[:cr] * r_ref[g, pl.ds(r0, cr), :]
            o_ref[g, pl.ds(r0, cr), :] = o_all.astype(o_ref.dtype)
        return carry

    lax.fori_loop(0, nchunk, chunk, 0, unroll=4 if nchunk % 4 == 0 else 1)

    @pl.when(ti == pl.num_programs(1) - 1)
    def _():
        sfin_ref[...] = state[...]


def _gla(gqkv, gla, gr, s0, gnorm, bg, tg, out_dtype, precise):
    if tg % GLA_CHUNK == 0:
        G, cr, cp, tt = 2 if bg % 2 == 0 else 1, GLA_CHUNK, GLA_CHUNK, min(tg, 512)
    else:
        assert tg % SUBLANES == 0 and tg <= 2 * SUBLANES
        G, cr, cp, tt = SUBLANES, tg, 2 * SUBLANES, tg
    assert bg % G == 0 and tg % tt == 0
    blk = lambda n: pl.BlockSpec((G, tt, n), lambda b, t: (b, t, 0))
    sblk = pl.BlockSpec((G, GLA_H, GLA_DK, GLA_DV), lambda b, t: (b, 0, 0, 0))
    kern = functools.partial(_gla_kernel, G=G, cr=cr, cp=cp, nchunk=tt // cr, precise=precise)
    o, sfin = pl.pallas_call(
        kern, grid=(bg // G, tg // tt),
        in_specs=[blk(1024), blk(256), blk(512), sblk, _const_spec(gnorm.shape)],
        out_specs=(blk(512), sblk),
        out_shape=(jax.ShapeDtypeStruct((bg, tg, 512), out_dtype),
                   jax.ShapeDtypeStruct((bg, GLA_H, GLA_DK, GLA_DV), F32)),
        scratch_shapes=[pltpu.VMEM((G, GLA_H, GLA_DK, GLA_DV), F32)],
        compiler_params=_params('parallel', 'arbitrary'), name='gla')(
            gqkv.reshape(bg, tg, 1024), gla.reshape(bg, tg, 256), gr.reshape(bg, tg, 512), s0, gnorm)
    return o.reshape(bg * tg, 512), sfin


def _conv_kernel(bch_ref, buf_ref, w_ref, o_ref, st_ref, prev, *, G, tt):
    @pl.when(pl.program_id(1) == 0)
    def _():
        prev[...] = buf_ref[...]

    w = w_ref[...]
    rowi = lax.broadcasted_iota(jnp.int32, (tt, CONV_D), 0)
    for g in range(G):
        bch = bch_ref[g]
        u = bch[:, CONV_D:2 * CONV_D] * bch[:, 2 * CONV_D:]
        p = prev[g]
        u1 = jnp.where(rowi == 0, p[1:2], pltpu.roll(u, 1, 0))
        u2 = jnp.where(rowi == 0, p[0:1], jnp.where(rowi == 1, p[1:2], pltpu.roll(u, 2, 0)))
        y = w[0:1] * u2 + w[1:2] * u1 + w[2:3] * u
        o_ref[g] = (bch[:, :CONV_D] * y).astype(o_ref.dtype)
        tail = u[tt - (CONV_W - 1):tt]
        prev[g] = tail
        st_ref[g] = tail


def _conv(bch, buf, conv_w, bg, tg, out_dtype):
    if tg >= 512:
        G, tt = 1, 512
    else:
        G, tt = min(bg, 16), tg
    assert bg % G == 0 and tg % tt == 0 and tt >= CONV_W - 1
    sblk = pl.BlockSpec((G, CONV_W - 1, CONV_D), lambda b, t: (b, 0, 0))
    o, st = pl.pallas_call(
        functools.partial(_conv_kernel, G=G, tt=tt), grid=(bg // G, tg // tt),
        in_specs=[pl.BlockSpec((G, tt, 3 * CONV_D), lambda b, t: (b, t, 0)), sblk,
                  _const_spec(conv_w.shape)],
        out_specs=(pl.BlockSpec((G, tt, CONV_D), lambda b, t: (b, t, 0)), sblk),
        out_shape=(jax.ShapeDtypeStruct((bg, tg, CONV_D), out_dtype),
                   jax.ShapeDtypeStruct((bg, CONV_W - 1, CONV_D), F32)),
        scratch_shapes=[pltpu.VMEM((G, CONV_W - 1, CONV_D), F32)],
        compiler_params=_params('parallel', 'arbitrary'), name='conv')(
            bch.reshape(bg, tg, 3 * CONV_D), buf, conv_w)
    return o.reshape(bg * tg, CONV_D), st


def _fox_prompt_kernel(*refs, tq, tk, hb, precise, q_off):
    if precise:
        q_ref, k_ref, v_ref, ql_ref, kl_ref, vl_ref, o_ref, m_scr, acc_scr, sa_scr, sb_scr = refs
    else:
        q_ref, k_ref, v_ref, o_ref, m_scr, acc_scr, sa_scr, sb_scr = refs
    qi = pl.program_id(2) + q_off
    nfull = (qi * tq) // tk
    row = lax.broadcasted_iota(jnp.int32, (tq, tk), 0)
    col = lax.broadcasted_iota(jnp.int32, (tq, tk), 1)
    m_scr[...] = jnp.full_like(m_scr, -jnp.inf)
    acc_scr[...] = jnp.zeros_like(acc_scr)

    def scores(s_ref, t):
        k0 = pl.multiple_of(t * tk, tk)
        for h in range(hb):
            ls = slice(h * AUG, (h + 1) * AUG)
            kt = k_ref[pl.ds(k0, tk), ls]
            s = _dot_nt(q_ref[:, ls], kt)
            if precise:
                s = s + _dot_nt(ql_ref[:, ls], kt) + _dot_nt(q_ref[:, ls], kl_ref[pl.ds(k0, tk), ls])
            s_ref[h] = s

    def update(s_ref, t, masked):
        k0 = pl.multiple_of(t * tk, tk)
        for h in range(hb):
            ls = slice(h * AUG, (h + 1) * AUG)
            s = s_ref[h]
            if masked:
                s = jnp.where(col + k0 <= row + qi * tq, s, -jnp.inf)
            m_prev = m_scr[h]
            m_new = jnp.maximum(m_prev, jnp.max(s, axis=-1, keepdims=True))
            alpha = jnp.exp(m_prev - m_new)
            p = jnp.exp(s - jnp.concatenate([m_new] * (tk // LANES), axis=1))
            pb = p.astype(BF16)
            vt = v_ref[pl.ds(k0, tk), ls]
            pv = _dot(pb, vt)
            if precise:
                pv = pv + _dot((p - pb.astype(F32)).astype(BF16), vt) + _dot(pb, vl_ref[pl.ds(k0, tk), ls])
            acc_scr[h] = alpha * acc_scr[h] + pv
            m_scr[h] = m_new

    npair = nfull // 2
    scores(sa_scr, 0)

    def body(i, carry):
        scores(sb_scr, 2 * i + 1)
        update(sa_scr, 2 * i, False)
        scores(sa_scr, 2 * i + 2)
        update(sb_scr, 2 * i + 1, False)
        return carry

    lax.fori_loop(0, npair, body, 0)

    @pl.when(nfull == 2 * npair)
    def _():
        update(sa_scr, nfull, True)

    @pl.when(nfull != 2 * npair)
    def _():
        scores(sb_scr, nfull)
        update(sa_scr, nfull - 1, False)
        update(sb_scr, nfull, True)
    lane = lax.broadcasted_iota(jnp.int32, (1, AUG), 1)
    outs = []
    for h in range(0, hb, 2):
        o = []
        for hh in (h, h + 1):
            acc = acc_scr[hh]
            o.append(acc / acc[:, FOX_HD:FOX_HD + 1])
        outs.append(jnp.where(lane < FOX_HD, o[0], pltpu.roll(o[1], FOX_HD, 1)))
    o_ref[...] = jnp.concatenate(outs, axis=1).astype(o_ref.dtype)


def _fox_prompt(qkv, bg, tg, n_run=None):
    tq = min(tg, 256)
    tk = min(tg, 512)
    hb = 4
    nq = tg // tq
    precise = len(qkv) == 6
    n_run = nq if n_run is None else n_run
    q_spec = pl.BlockSpec((tq, hb * AUG), lambda b, p, i: (b * nq + (nq - n_run) + i, p))
    kv_spec = pl.BlockSpec((tg, hb * AUG), lambda b, p, i: (b, p), pipeline_mode=pl.Buffered(1))
    specs = [q_spec, kv_spec, kv_spec]
    return pl.pallas_call(
        functools.partial(_fox_prompt_kernel, tq=tq, tk=tk, hb=hb, precise=precise, q_off=nq - n_run),
        grid=(bg, FOX_H // hb, n_run),
        in_specs=specs * (2 if precise else 1),
        out_specs=pl.BlockSpec((tq, hb * FOX_HD), lambda b, p, i: (b * n_run + i, p)),
        out_shape=jax.ShapeDtypeStruct((bg * n_run * tq, BRANCH_W), F32 if precise else BF16),
        scratch_shapes=[pltpu.VMEM((hb, tq, AUG), F32)] * 2 + [pltpu.VMEM((hb, tq, tk), F32)] * 2,
        compiler_params=_params('parallel', 'parallel', 'arbitrary'), name='fox_prompt')(*qkv)


def _fox_sample_kernel(pt_ref, q_ref, kn_ref, vn_ref, lfn_ref, *refs, P, tq, single):
    del pt_ref
    k_refs, v_refs, lf_refs = refs[:P], refs[P:2 * P], refs[2 * P:3 * P]
    o_ref, qbd, m_scr, l_scr, acc_scr, carry = refs[3 * P:]
    j = pl.program_id(1)
    nrow = tq * FOX_H
    hrow = lax.broadcasted_iota(jnp.int32, (FOX_H, BRANCH_W), 0)
    hcol = lax.broadcasted_iota(jnp.int32, (FOX_H, BRANCH_W), 1)
    hmask = (hcol // FOX_HD) == hrow

    @pl.when(j == 0)
    def _():
        q = q_ref[...]
        qbd[...] = jnp.concatenate(
            [jnp.where(hmask, jnp.broadcast_to(q[t:t + 1, :], (FOX_H, BRANCH_W)), 0.0) for t in range(tq)],
            axis=0).astype(BF16)
        m_scr[...] = jnp.full_like(m_scr, -jnp.inf)
        l_scr[...] = jnp.zeros_like(l_scr)
        acc_scr[...] = jnp.zeros_like(acc_scr)
        carry[...] = jnp.zeros_like(carry)

    def update(qks, lfs, own_last, pvs):
        ss = []
        cur = carry[...]
        for i, (qk, (pre, tot)) in enumerate(zip(qks, [_lane_prefix(lf) for lf in lfs])):
            sb = qk - jnp.concatenate([cur + pre] * tq, axis=0)
            if own_last and i == len(qks) - 1:
                key = lax.broadcasted_iota(jnp.int32, (nrow, PAGE_SIZE), 1)
                tok = lax.broadcasted_iota(jnp.int32, (nrow, PAGE_SIZE), 0) // FOX_H
                sb = jnp.where(key <= tok, sb, -jnp.inf)
            ss.append(sb)
            cur = cur + tot
        carry[...] = cur
        s = jnp.concatenate(ss, axis=1)
        m_prev = m_scr[...]
        m_new = jnp.maximum(m_prev, jnp.max(s, axis=-1, keepdims=True))
        alpha = jnp.exp(m_prev - m_new)
        p = jnp.exp(s - jnp.concatenate([m_new] * len(qks), axis=1))
        l_scr[...] = alpha * l_scr[...] + jnp.sum(p, axis=-1, keepdims=True)
        pb = p.astype(BF16)
        pv = None
        for i, f in enumerate(pvs):
            term = f(pb[:, i * PAGE_SIZE:(i + 1) * PAGE_SIZE])
            pv = term if pv is None else pv + term
        acc_scr[...] = jnp.concatenate([alpha] * (BRANCH_W // LANES), axis=1) * acc_scr[...] + pv
        m_scr[...] = m_new

    q_bd = qbd[...]
    kts = [k_refs[i][...].reshape(BRANCH_W, PAGE_SIZE).astype(BF16) for i in range(P)]
    vts = [v_refs[i][...].reshape(BRANCH_W, PAGE_SIZE).astype(BF16) for i in range(P)]
    page_qk = [_dot(q_bd, kt) for kt in kts]
    page_lf = [lf_refs[i][...] for i in range(P)]
    page_pv = [lambda p, vt=vt: _dot_nt(p, vt) for vt in vts]

    def own_block():
        zpad = jnp.zeros((PAGE_SIZE - tq, BRANCH_W), F32)
        kn = jnp.concatenate([kn_ref[...], zpad], axis=0).astype(BF16)
        vn = jnp.concatenate([vn_ref[...], zpad], axis=0).astype(BF16)
        return _dot_nt(q_bd, kn), lfn_ref[0], lambda p: _dot(p, vn)

    def finish():
        o = acc_scr[...] / jnp.concatenate([l_scr[...]] * (BRANCH_W // LANES), axis=1)
        rows = [jnp.sum(jnp.where(hmask, o[t * FOX_H:(t + 1) * FOX_H], 0.0), axis=0, keepdims=True)
                for t in range(tq)]
        o_ref[...] = jnp.concatenate(rows, axis=0)

    if single:
        qk_n, lf_n, pv_n = own_block()
        update(page_qk + [qk_n], page_lf + [lf_n], True, page_pv + [pv_n])
        finish()
    else:
        update(page_qk, page_lf, False, page_pv)

        @pl.when(j == pl.num_programs(1) - 1)
        def _():
            qk_n, lf_n, pv_n = own_block()
            update([qk_n], [lf_n], True, [pv_n])
            finish()


def _fox_sample(fq, fk32, fv32, lft, layer, k_cache, v_cache, lf_cache, page_table, db, tq):
    n_pages = page_table.shape[1]
    P = next(p for p in (16, 8, 4, 2, 1) if n_pages % p == 0)
    assert tq == SUBLANES
    kt = jnp.transpose(k_cache, (0, 1, 3, 4, 2))
    vt = jnp.transpose(v_cache, (0, 1, 3, 4, 2))
    lfp = jnp.swapaxes(lf_cache, 2, 3)
    lfn = jnp.swapaxes(lft.reshape(FOX_H, db, tq), 0, 1)
    lfn = jnp.pad(lfn, ((0, 0), (0, 0), (0, PAGE_SIZE - tq)))
    tok = lambda: pl.BlockSpec((tq, BRANCH_W), lambda b, j, pt: (b, 0))
    page = lambda i: pl.BlockSpec((None, None, FOX_H, FOX_HD, PAGE_SIZE),
                                  lambda b, j, pt, i=i: (layer, pt[b, j * P + i], 0, 0, 0))
    lfpage = lambda i: pl.BlockSpec((None, None, FOX_H, PAGE_SIZE),
                                    lambda b, j, pt, i=i: (layer, pt[b, j * P + i], 0, 0))
    nrow = tq * FOX_H
    return pl.pallas_call(
        functools.partial(_fox_sample_kernel, P=P, tq=tq, single=n_pages == P),
        grid_spec=pltpu.PrefetchScalarGridSpec(
            num_scalar_prefetch=1, grid=(db, n_pages // P),
            in_specs=[tok(), tok(), tok(), pl.BlockSpec((1, FOX_H, PAGE_SIZE), lambda b, j, pt: (b, 0, 0))]
                     + [page(i) for i in range(P)] + [page(i) for i in range(P)]
                     + [lfpage(i) for i in range(P)],
            out_specs=tok(),
            scratch_shapes=[pltpu.VMEM((nrow, BRANCH_W), BF16), pltpu.VMEM((nrow, LANES), F32),
                            pltpu.VMEM((nrow, LANES), F32), pltpu.VMEM((nrow, BRANCH_W), F32),
                            pltpu.VMEM((FOX_H, LANES), F32)]),
        out_shape=jax.ShapeDtypeStruct((db * tq, BRANCH_W), F32),
        compiler_params=_params('parallel', 'arbitrary'), name='fox_sample')(
            page_table, fq, fk32, fv32, lfn, *([kt] * P), *([vt] * P), *([lfp] * P))


def _merge_kernel(x_ref, og_ref, of_ref, oc_ref, gate_ref, wb_ref, wo_ref, g_ref, b_ref, y_ref, *, precise):
    m = None
    for jb, o_ref in enumerate((og_ref, of_ref, oc_ref)):
        gj = _sigmoid(gate_ref[:, jb * D_MODEL:(jb + 1) * D_MODEL])
        term = gj * _mm(_pieces(o_ref[...].astype(F32), precise), wb_ref.at[:, jb])
        m = term if m is None else m + term
    y = DEEPNORM_ALPHA * x_ref[...] + _mm(_pieces(m, precise), wo_ref)
    y_ref[...] = _layernorm(y, g_ref[...], b_ref[...])


def _merge(x, o_gla, o_fox, o_conv, gate, w, precise):
    rows = x.shape[0]
    tm = _row_tile(rows, 256)
    row = lambda n: pl.BlockSpec((tm, n), lambda i: (i, 0))
    consts = [w['wb'], w['wo'], w['ln1_g'], w['ln1_b']]
    return pl.pallas_call(
        functools.partial(_merge_kernel, precise=precise), grid=(rows // tm,),
        in_specs=[row(D_MODEL), row(BRANCH_W), row(BRANCH_W), row(BRANCH_W), row(N_BRANCH * D_MODEL)]
                 + [_const_spec(a.shape) for a in consts],
        out_specs=row(D_MODEL), out_shape=jax.ShapeDtypeStruct((rows, D_MODEL), F32),
        compiler_params=_params('parallel'), name='merge')(x, o_gla, o_fox, o_conv, gate, *consts)


def _route(probs):
    rows = [probs[i:i + 1, :] for i in range(N_EXPERTS)]
    gscore = []
    for g in range(N_GROUPS):
        r0, r1, r2, r3 = rows[EXP_PER_GROUP * g:EXP_PER_GROUP * (g + 1)]
        a, b = jnp.maximum(r0, r1), jnp.minimum(r0, r1)
        c, d = jnp.maximum(r2, r3), jnp.minimum(r2, r3)
        gscore.append(jnp.maximum(a, c) + jnp.maximum(jnp.minimum(a, c), jnp.maximum(b, d)))
    best, gsel = gscore[0], jnp.zeros_like(gscore[0], dtype=jnp.int32)
    for g in range(1, N_GROUPS):
        upd = gscore[g] > best
        best = jnp.where(upd, gscore[g], best)
        gsel = jnp.where(upd, g, gsel)
    v = []
    for i in range(EXP_PER_GROUP):
        vi = rows[(N_GROUPS - 1) * EXP_PER_GROUP + i]
        for g in range(N_GROUPS - 2, -1, -1):
            vi = jnp.where(gsel == g, rows[g * EXP_PER_GROUP + i], vi)
        v.append(vi)
    v1, i1 = v[0], jnp.zeros_like(gsel)
    for i in range(1, EXP_PER_GROUP):
        upd = v[i] > v1
        v1 = jnp.where(upd, v[i], v1)
        i1 = jnp.where(upd, i, i1)
    v2, i2 = jnp.full_like(v1, -1.0), jnp.full_like(gsel, -1)
    for i in range(EXP_PER_GROUP):
        cand = jnp.where(i1 == i, -1.0, v[i])
        upd = cand > v2
        v2 = jnp.where(upd, cand, v2)
        i2 = jnp.where(upd, i, i2)
    den = v1 + v2
    w1, w2 = v1 / den, v2 / den
    out = []
    for g in range(N_GROUPS):
        for i in range(EXP_PER_GROUP):
            wi = jnp.where(i1 == i, w1, jnp.where(i2 == i, w2, 0.0))
            out.append(jnp.where(gsel == g, wi, 0.0))
    return out


def _moe_kernel(x_ref, wrt_ref, br_ref, wg_ref, wu_ref, wd_ref, g_ref, b_ref, y_ref):
    x = x_ref[...]
    tm = x.shape[0]
    xb = x.astype(BF16)
    x_lo = (x - xb.astype(F32)).astype(BF16)
    wrt = wrt_ref[...]
    w_hi = wrt.astype(BF16)
    w_lo = (wrt - w_hi.astype(F32)).astype(BF16)
    logits = _dot_nt(w_hi, xb) + _dot_nt(w_hi, x_lo) + _dot_nt(w_lo, xb) + br_ref[...]
    e = jnp.exp(logits - jnp.max(logits, axis=0, keepdims=True))
    probs = e / jnp.sum(e, axis=0, keepdims=True)
    gate_rows = _route(probs)
    gate_t = jnp.concatenate(gate_rows + [jnp.zeros((LANES - N_EXPERTS, tm), F32)], axis=0)
    gate = gate_t.T
    acc = jnp.zeros((tm, D_MODEL), F32)
    for ex in range(N_EXPERTS):
        hg = _dot(xb, wg_ref[ex])
        hu = _dot(xb, wu_ref[ex])
        h = (hg * _sigmoid(hg)) * hu * gate[:, ex:ex + 1]
        acc = acc + _dot(h.astype(BF16), wd_ref[ex])
    y = DEEPNORM_ALPHA * x + acc
    y_ref[...] = _layernorm(y, g_ref[...], b_ref[...])


def _moe(x, w):
    rows = x.shape[0]
    tm = _row_tile(rows, 512)
    row = pl.BlockSpec((tm, D_MODEL), lambda i: (i, 0))
    consts = [w['wrt'], w['br'], w['weg'], w['weu'], w['wed'], w['ln2_g'], w['ln2_b']]
    return pl.pallas_call(
        _moe_kernel, grid=(rows // tm,),
        in_specs=[row] + [_const_spec(a.shape) for a in consts],
        out_specs=row, out_shape=jax.ShapeDtypeStruct((rows, D_MODEL), F32),
        compiler_params=_params('parallel'), name='moe')(x, *consts)


def _prep_weights(l, precise, w_in, w_gla_a2, b_gla_a, gla_norm_g, b_fox_f, conv_w, w_branch, w_out, ln1_g, ln1_b,
                  w_router, b_router, w_e_gate, w_e_up, w_e_down, ln2_g, ln2_b):
    wi = w_in[l]
    off = {}
    o = 0
    for name, n in (('gla_q', 256), ('gla_k', 256), ('gla_v', 512), ('gla_a', GLA_LR), ('gla_r', 512),
                    ('fox_q', 512), ('fox_k', 512), ('fox_v', 512), ('fox_f', FOX_H),
                    ('conv', 3 * CONV_D), ('gate', N_BRANCH * D_MODEL)):
        off[name] = (o, o + n)
        o += n
    col = lambda name: wi[:, off[name][0]:off[name][1]]
    padc = lambda a, n: jnp.pad(a, ((0, 0), (0, n - a.shape[1])))
    row2 = lambda a: a.reshape(1, -1)

    def bf(a):
        hi = a.astype(BF16)
        return jnp.stack([hi, (a - hi.astype(F32)).astype(BF16)]) if precise else hi[None]

    ff = col('fox_f')
    return {
        'wqkv': bf(jnp.concatenate([col('gla_q') * (GLA_DK ** -0.5), col('gla_k'), col('gla_v')], axis=1)),
        'wa': bf(padc(col('gla_a'), LANES)),
        'wa2': bf(jnp.pad(w_gla_a2[l], ((0, LANES - GLA_LR), (0, 0)))),
        'ba': row2(b_gla_a[l]),
        'wr': bf(col('gla_r')),
        'wfq': bf((col('fox_q') * (FOX_HD ** -0.5))),
        'wfk': bf(col('fox_k')),
        'wfv': bf(col('fox_v')),
        'wfkt': bf(col('fox_k').T),
        'wfvt': bf(col('fox_v').T),
        'wff': bf(padc(ff, LANES)),
        'bff': padc(row2(b_fox_f[l]), LANES),
        'wfft': bf(jnp.pad(ff.T, ((0, 2 * SUBLANES - FOX_H), (0, 0)))),
        'bfft': b_fox_f[l].reshape(FOX_H, 1),
        'pad_ones': (jnp.arange(BRANCH_W)[None, :] % FOX_HD < jnp.arange(1, SUBLANES + 1)[:, None]).astype(F32),
        'wc': bf(col('conv')),
        'wg': bf(col('gate')),
        'gnorm': row2(gla_norm_g[l]),
        'conv_w': conv_w[l],
        'wb': bf(w_branch[l]),
        'wo': bf(w_out[l]),
        'ln1_g': row2(ln1_g[l]), 'ln1_b': row2(ln1_b[l]),
        'wrt': w_router.T, 'br': b_router.reshape(N_EXPERTS, 1),
        'weg': w_e_gate[l].astype(BF16), 'weu': w_e_up[l].astype(BF16), 'wed': w_e_down[l].astype(BF16),
        'ln2_g': row2(ln2_g[l]), 'ln2_b': row2(ln2_b[l]),
    }


TAIL = 256
SPLICE = 128


def _layer(x, w, bg, tg, gla_s0, conv_buf, pools, act_dtype, precise):
    prompt = pools is None
    precise = precise and prompt and tg >= 2 * TAIL
    gqkv, gla, gr, fk, fv, lf, *fox_in = _proj1(x, w, prompt, tg, precise)
    bch, gate = _proj2(x, w, False)
    o_gla, gla_state = _gla(gqkv, gla, gr, gla_s0, w['gnorm'], bg, tg, F32 if precise else act_dtype, precise)
    if prompt:
        o_fox = _fox_prompt(fox_in[:3], bg, tg)
        caches = (fk.reshape(bg, FOX_H, FOX_HD, tg), fv.reshape(bg, FOX_H, FOX_HD, tg), lf)
    else:
        o_fox = _fox_sample(fox_in[0], fk, fv, fox_in[1], *pools, bg, tg)
        caches = (fk.reshape(bg, tg, FOX_H, FOX_HD), fv.reshape(bg, tg, FOX_H, FOX_HD), lf.reshape(bg, tg, FOX_H))
    o_conv, conv_state = _conv(bch, conv_buf, w['conv_w'], bg, tg, act_dtype)
    y = _moe(_merge(x, o_gla, o_fox, o_conv, gate, w, False), w)
    if precise:
        n_t = TAIL // min(tg, 256)
        tail = lambda a: a.reshape(bg, tg, a.shape[-1])[:, tg - TAIL:].reshape(bg * TAIL, a.shape[-1])
        xt = tail(x)
        bch_t, gate_t = _proj2(xt, w, True)
        o_fox_t = _fox_prompt(fox_in, bg, tg, n_run=n_t)
        o_conv_t, _ = _conv(bch_t, jnp.zeros_like(conv_buf), w['conv_w'], bg, TAIL, F32)
        y_t = _moe(_merge(xt, tail(o_gla), o_fox_t, o_conv_t, gate_t, w, True), w)
        y = y.reshape(bg, tg, D_MODEL).at[:, tg - SPLICE:].set(
            y_t.reshape(bg, TAIL, D_MODEL)[:, TAIL - SPLICE:]).reshape(bg * tg, D_MODEL)
    return y, caches + (gla_state, conv_state)


def kernel(x_prompt, x_sample, cache_fox_k, cache_fox_v, cache_fox_logf, state_gla, state_conv, page_table, w_in, w_gla_a2, b_gla_a, gla_norm_g, b_fox_f, conv_w, w_branch, w_out, ln1_g, ln1_b, w_router, b_router, w_e_gate, w_e_up, w_e_down, ln2_g, ln2_b):
    B, T, _ = x_prompt.shape
    DB, TS, _ = x_sample.shape
    xp = x_prompt.reshape(B * T, D_MODEL)
    xs = x_sample.reshape(DB * TS, D_MODEL)
    depth = w_in.shape[0]
    st_p, st_s = [], []
    for l in range(depth):
        precise = l < depth - 1
        w = _prep_weights(l, precise, w_in, w_gla_a2, b_gla_a, gla_norm_g, b_fox_f, conv_w, w_branch, w_out,
                          ln1_g, ln1_b, w_router, b_router, w_e_gate, w_e_up, w_e_down, ln2_g, ln2_b)
        xp, sp = _layer(xp, w, B, T, jnp.zeros((B, GLA_H, GLA_DK, GLA_DV), F32),
                        jnp.zeros((B, CONV_W - 1, CONV_D), F32), None, BF16, precise)
        xs, ss = _layer(xs, w, DB, TS, state_gla[l], state_conv[l],
                        (l, cache_fox_k, cache_fox_v, cache_fox_logf, page_table), F32, False)
        st_p.append(sp)
        st_s.append(ss)
    stack = lambda sts, i: jnp.stack([s[i] for s in sts])
    return (xp.reshape(B, T, D_MODEL), xs.reshape(DB, TS, D_MODEL),
            jnp.transpose(stack(st_p, 0), (0, 1, 4, 2, 3)), jnp.transpose(stack(st_p, 1), (0, 1, 4, 2, 3)),
            jnp.swapaxes(stack(st_p, 2), 2, 3), stack(st_p, 3), stack(st_p, 4),
            stack(st_s, 0), stack(st_s, 1), stack(st_s, 2), stack(st_s, 3), stack(st_s, 4))
```

```python
import functools

import jax
import jax.numpy as jnp
from jax import lax
from jax.experimental import pallas as pl
from jax.experimental.pallas import tpu as pltpu

F32 = jnp.float32
BF16 = jnp.bfloat16

D_MODEL = 1024
DEPTH = 2
PAGE_SIZE = 128
BRANCH_W = D_MODEL // 2
GLA_H = 4
GLA_DK = BRANCH_W // (2 * GLA_H)
GLA_DV = BRANCH_W // GLA_H
GLA_LR = 16
GLA_TAU = 16.0
GLA_CHUNK = 64
FOX_HD = 64
FOX_H = BRANCH_W // FOX_HD
CONV_D = BRANCH_W
CONV_W = 3
N_BRANCH = 3
N_EXPERTS = 16
N_GROUPS = 4
EXP_PER_GROUP = N_EXPERTS // N_GROUPS
D_EXPERT = D_MODEL // 4
LN_EPS = 1e-5
DEEPNORM_ALPHA = (2 * DEPTH) ** 0.25

LANES = 128
SUBLANES = 8
VMEM_LIMIT = 56 * 1024 * 1024

_NT = (((1,), (1,)), ((), ()))
_TN = (((0,), (0,)), ((), ()))


def _dot(a, b):
    return jnp.dot(a, b, preferred_element_type=F32)


def _dot_nt(a, b):
    return lax.dot_general(a, b, _NT, preferred_element_type=F32)


def _dot_tn(a, b):
    return lax.dot_general(a, b, _TN, preferred_element_type=F32)


def _split3(x):
    hi = x.astype(BF16)
    r1 = x - hi.astype(F32)
    mid = r1.astype(BF16)
    lo = (r1 - mid.astype(F32)).astype(BF16)
    return hi, mid, lo


def _pieces(x, precise):
    hi = x.astype(BF16)
    if not precise:
        return (hi,)
    return hi, (x - hi.astype(F32)).astype(BF16)


def _mm(xs, w_ref):
    y = _dot(xs[0], w_ref[0])
    if len(xs) == 2:
        y = y + _dot(xs[1], w_ref[0]) + _dot(xs[0], w_ref[1])
    return y


def _mm_nt(w_ref, xs):
    y = _dot_nt(w_ref[0], xs[0])
    if len(xs) == 2:
        y = y + _dot_nt(w_ref[0], xs[1]) + _dot_nt(w_ref[1], xs[0])
    return y


def _pdot(dot, a, b, precise):
    ap, bp = _pieces(a, precise), _pieces(b, precise)
    y = dot(ap[0], bp[0])
    if precise:
        y = y + dot(ap[1], bp[0]) + dot(ap[0], bp[1])
    return y


def _log_sigmoid(x):
    return jnp.minimum(x, 0.0) - jnp.log1p(jnp.exp(-jnp.abs(x)))


def _sigmoid(x):
    return 1.0 / (1.0 + jnp.exp(-x))


def _layernorm(y, g, b):
    mu = jnp.mean(y, axis=-1, keepdims=True)
    d = y - mu
    var = jnp.mean(d * d, axis=-1, keepdims=True)
    return d * lax.rsqrt(var + LN_EPS) * g + b


def _lane_prefix(x):
    r = lax.broadcasted_iota(jnp.int32, (LANES, 2 * LANES), 0)
    c = lax.broadcasted_iota(jnp.int32, (LANES, 2 * LANES), 1)
    u = jnp.where(r <= c, 1.0, 0.0).astype(BF16)
    hi, mid, lo = _split3(x)
    y = _dot(hi, u) + _dot(mid, u) + _dot(lo, u)
    return y[:, :LANES], y[:, LANES:]


def _const_spec(shape):
    nd = len(shape)
    return pl.BlockSpec(shape, lambda *_: (0,) * nd, pipeline_mode=pl.Buffered(1))


def _params(*sem):
    return pltpu.CompilerParams(dimension_semantics=sem, vmem_limit_bytes=VMEM_LIMIT)


def _row_tile(rows, want):
    t = min(rows, want)
    assert rows % t == 0
    return t


AUG = 2 * FOX_HD
AUG_W = FOX_H * AUG


def _mm_rows(tail, xs, w_ref, store):
    if tail is None or len(xs) == 1:
        store(_mm(xs, w_ref))
        return

    @pl.when(tail)
    def _():
        store(_mm(xs, w_ref))

    @pl.when(jnp.logical_not(tail))
    def _():
        store(_mm(xs[:1], w_ref))


def _gla_cols(xs, tail, wqkv_ref, wa_ref, wa2_ref, ba_ref, wr_ref, gqkv_ref, gla_ref, gr_ref):
    nq = GLA_H * GLA_DK

    def put_q(y):
        gqkv_ref[:, :nq] = y

    def put_r(r):
        gr_ref[...] = r * _sigmoid(r)

    _mm_rows(tail, xs, wqkv_ref.at[:, :, :nq], put_q)
    gqkv_ref[:, nq:] = _mm(xs, wqkv_ref.at[:, :, nq:])
    a_pre = _mm(_pieces(_mm(xs, wa_ref), len(xs) == 2), wa2_ref) + ba_ref[...]
    gla_ref[...] = _log_sigmoid(a_pre) * (1.0 / GLA_TAU)
    _mm_rows(tail, xs, wr_ref, put_r)


def _proj1_prompt_kernel(x_ref, wqkv_ref, wa_ref, wa2_ref, ba_ref, wr_ref, wfk_ref, wfv_ref, wff_ref, bff_ref,
                         wfft_ref, bfft_ref, wqa_ref, wka_ref, wva_ref, pad_ref,
                         gqkv_ref, gla_ref, gr_ref, fk_ref, fv_ref, lf_ref, qa_ref, ka_ref, va_ref,
                         *rest, tiles_per_seq, tail_tiles, precise):
    tm = x_ref.shape[0]
    lo_refs, carry = rest[:-1], rest[-1]

    @pl.when(pl.program_id(0) % tiles_per_seq == 0)
    def _():
        carry[...] = jnp.zeros_like(carry)

    xs = _pieces(x_ref[...], precise)
    tail = (pl.program_id(0) % tiles_per_seq >= tiles_per_seq - tail_tiles) if precise else None
    _gla_cols(xs, tail, wqkv_ref, wa_ref, wa2_ref, ba_ref, wr_ref, gqkv_ref, gla_ref, gr_ref)
    fk_ref[...] = _mm_nt(wfk_ref, xs[:1])
    fv_ref[...] = _mm_nt(wfv_ref, xs[:1])
    lf_ref[...] = _log_sigmoid(_mm_nt(wfft_ref, xs[:1])[:FOX_H] + bfft_ref[...])
    lf = _log_sigmoid(_mm(xs, wff_ref) + bff_ref[...])
    ri = lax.broadcasted_iota(jnp.int32, (tm, tm), 0)
    ci = lax.broadcasted_iota(jnp.int32, (tm, tm), 1)
    lmat = jnp.where(ci <= ri, 1.0, 0.0).astype(BF16)
    hi, mid, lo = _split3(lf)
    c = _dot(lmat, hi) + _dot(lmat, mid) + _dot(lmat, lo) + carry[0:1, :]
    carry[...] = jnp.broadcast_to(c[tm - 1:tm, :], carry.shape)
    pr = lax.broadcasted_iota(jnp.int32, (LANES, BRANCH_W), 0)
    pc = lax.broadcasted_iota(jnp.int32, (LANES, BRANCH_W), 1)
    kpad = None
    for j, part in enumerate(_split3(-c)):
        term = _dot(part, jnp.where(pc == pr * FOX_HD + j, 1.0, 0.0).astype(BF16))
        kpad = term if kpad is None else kpad + term
    ones = lambda n: jnp.broadcast_to(pad_ref[n - 1:n, :], (tm, BRANCH_W))

    def put(i, val, pad, hi_ref):
        parts = []
        for h in range(FOX_H):
            sl = slice(h * FOX_HD, (h + 1) * FOX_HD)
            parts += [val[:, sl], pad[:, sl]]
        aug = jnp.concatenate(parts, axis=1)
        hi = aug.astype(BF16)
        hi_ref[...] = hi
        if precise:
            lo_refs[i][...] = (aug - hi.astype(F32)).astype(BF16)

    put(1, _mm(xs, wka_ref), kpad, ka_ref)
    put(2, _mm(xs, wva_ref), ones(1), va_ref)
    _mm_rows(tail, xs, wqa_ref, lambda q: put(0, q, ones(3), qa_ref))


def _proj1_sample_kernel(x_ref, wqkv_ref, wa_ref, wa2_ref, ba_ref, wr_ref, wfk_ref, wfv_ref, wff_ref, bff_ref,
                         wfft_ref, bfft_ref, wfq_ref,
                         gqkv_ref, gla_ref, gr_ref, fk_ref, fv_ref, lf_ref, fq_ref, lft_ref):
    xs = _pieces(x_ref[...], False)
    _gla_cols(xs, None, wqkv_ref, wa_ref, wa2_ref, ba_ref, wr_ref, gqkv_ref, gla_ref, gr_ref)
    fk_ref[...] = _mm(xs, wfk_ref)
    fv_ref[...] = _mm(xs, wfv_ref)
    lf_ref[...] = _log_sigmoid(_mm(xs, wff_ref) + bff_ref[...])[:, :FOX_H]
    fq_ref[...] = _mm(xs, wfq_ref)
    lft_ref[...] = _log_sigmoid(_mm_nt(wfft_ref, xs)[:FOX_H] + bfft_ref[...])


def _proj1(x, w, prompt, tg, precise):
    rows = x.shape[0]
    tm = _row_tile(rows, 256)
    row = lambda n: pl.BlockSpec((tm, n), lambda i: (i, 0))
    common = ('wqkv', 'wa', 'wa2', 'ba', 'wr', 'wfkt' if prompt else 'wfk', 'wfvt' if prompt else 'wfv',
              'wff', 'bff', 'wfft', 'bfft')
    sds = lambda n, dt: jax.ShapeDtypeStruct((rows, n), dt)
    out_shape = [sds(1024, F32), sds(256, F32), sds(512, F32)]
    out_specs = [row(1024), row(256), row(512)]
    if prompt:
        assert tg % tm == 0
        tps = tg // tm
        names = common + ('wfq', 'wfk', 'wfv', 'pad_ones')
        kern = functools.partial(_proj1_prompt_kernel, tiles_per_seq=tps, tail_tiles=-(-TAIL // tm),
                                 precise=precise)
        tr = lambda n: pl.BlockSpec((None, n, tm), lambda i: (i // tps, 0, i % tps))
        trs = lambda n: jax.ShapeDtypeStruct((rows // tg, n, tg), F32)
        n_aug = 6 if precise else 3
        out_shape += [trs(512), trs(512), trs(FOX_H)] + [sds(AUG_W, BF16)] * n_aug
        out_specs += [tr(512), tr(512), tr(FOX_H)] + [row(AUG_W)] * n_aug
        scratch = [pltpu.VMEM((SUBLANES, LANES), F32)]
    else:
        names = common + ('wfq',)
        kern = _proj1_sample_kernel
        out_shape += [sds(512, F32), sds(512, F32), sds(FOX_H, F32), sds(512, F32),
                      jax.ShapeDtypeStruct((FOX_H, rows), F32)]
        out_specs += [row(512), row(512), row(FOX_H), row(512), pl.BlockSpec((FOX_H, tm), lambda i: (0, i))]
        scratch = []
    ws = [w[n] for n in names]
    return pl.pallas_call(
        kern, grid=(rows // tm,),
        in_specs=[row(D_MODEL)] + [_const_spec(a.shape) for a in ws],
        out_specs=out_specs, out_shape=out_shape, scratch_shapes=scratch,
        compiler_params=_params('arbitrary'), name='proj1_prompt' if prompt else 'proj1_sample')(x, *ws)


def _proj2_kernel(x_ref, wc_ref, bch_ref, *, precise):
    bch_ref[...] = _mm(_pieces(x_ref[...], precise), wc_ref)


def _proj2(x, w, precise):
    rows = x.shape[0]
    tm = _row_tile(rows, 256)
    row = lambda n: pl.BlockSpec((tm, n), lambda i: (i, 0))
    return pl.pallas_call(
        functools.partial(_proj2_kernel, precise=precise), grid=(rows // tm,),
        in_specs=[row(D_MODEL), _const_spec(w['wc'].shape)],
        out_specs=row(3 * CONV_D), out_shape=jax.ShapeDtypeStruct((rows, 3 * CONV_D), F32),
        compiler_params=_params('parallel'), name='proj2')(x, w['wc'])


def _gla_kernel(qkv_ref, la_ref, r_ref, s0_ref, g_ref, o_ref, sfin_ref, state, *, G, cr, cp, nchunk, precise):
    ti = pl.program_id(1)

    @pl.when(ti == 0)
    def _():
        state[...] = s0_ref[...]

    ri = lax.broadcasted_iota(jnp.int32, (cp, cp), 0)
    ci = lax.broadcasted_iota(jnp.int32, (cp, cp), 1)
    tril = ci <= ri
    lmat = jnp.where(tril, 1.0, 0.0).astype(BF16)
    ones = jnp.ones((cp, GLA_DV), BF16)
    gnorm = g_ref[...]

    def pad(a):
        if cp == cr:
            return a
        return jnp.concatenate([a, jnp.zeros((cp - cr, a.shape[1]), a.dtype)], axis=0)

    def chunk(c, carry):
        r0 = pl.multiple_of(c * cr, cr)
        for g in range(G):
            qkv = pad(qkv_ref[g, pl.ds(r0, cr), :])
            la = pad(la_ref[g, pl.ds(r0, cr), :])
            q = qkv[:, :GLA_H * GLA_DK]
            k = qkv[:, GLA_H * GLA_DK:2 * GLA_H * GLA_DK]
            v = qkv[:, 2 * GLA_H * GLA_DK:]
            la3 = _split3(la)
            b = _dot(lmat, la3[0]) + _dot(lmat, la3[1]) + _dot(lmat, la3[2])
            b_last = b[cp - 1:cp, :]
            q_in = q * jnp.exp(b)
            k_in = k * jnp.exp(-b)
            k_end = k * jnp.exp(b_last - b)
            outs = []
            for h in range(GLA_H):
                ks = slice(h * GLA_DK, (h + 1) * GLA_DK)
                vs = slice(h * GLA_DV, (h + 1) * GLA_DV)
                s_prev = state[g, h]
                att = jnp.where(tril, _pdot(_dot_nt, q_in[:, ks], k_in[:, ks], precise), 0.0)
                o = _pdot(_dot, att, v[:, vs], precise) + _pdot(_dot, q_in[:, ks], s_prev, precise)
                dlog = (_dot_tn(la3[0][:, ks], ones) + _dot_tn(la3[1][:, ks], ones)
                        + _dot_tn(la3[2][:, ks], ones))
                state[g, h] = jnp.exp(dlog) * s_prev + _pdot(_dot_tn, k_end[:, ks], v[:, vs], precise)
                ms = jnp.mean(o * o, axis=-1, keepdims=True)
                outs.append(o * lax.rsqrt(ms + LN_EPS) * gnorm[:, vs])
            o_all = jnp.concatenate(outs, axis=1)[:cr] * r_ref[g, pl.ds(r0, cr), :]
            o_ref[g, pl.ds(r0, cr), :] = o_all.astype(o_ref.dtype)
        return carry

    lax.fori_loop(0, nchunk, chunk, 0, unroll=4 if nchunk % 4 == 0 else 1)

    @pl.when(ti == pl.num_programs(1) - 1)
    def _():
        sfin_ref[...] = state[...]


def _gla(gqkv, gla, gr, s0, gnorm, bg, tg, out_dtype, precise):
    if tg % GLA_CHUNK == 0:
        G, cr, cp, tt = 2 if bg % 2 == 0 else 1, GLA_CHUNK, GLA_CHUNK, min(tg, 512)
    else:
        assert tg % SUBLANES == 0 and tg <= 2 * SUBLANES
        G, cr, cp, tt = SUBLANES, tg, 2 * SUBLANES, tg
    assert bg % G == 0 and tg % tt == 0
    blk = lambda n: pl.BlockSpec((G, tt, n), lambda b, t: (b, t, 0))
    sblk = pl.BlockSpec((G, GLA_H, GLA_DK, GLA_DV), lambda b, t: (b, 0, 0, 0))
    kern = functools.partial(_gla_kernel, G=G, cr=cr, cp=cp, nchunk=tt // cr, precise=precise)
    o, sfin = pl.pallas_call(
        kern, grid=(bg // G, tg // tt),
        in_specs=[blk(1024), blk(256), blk(512), sblk, _const_spec(gnorm.shape)],
        out_specs=(blk(512), sblk),
        out_shape=(jax.ShapeDtypeStruct((bg, tg, 512), out_dtype),
                   jax.ShapeDtypeStruct((bg, GLA_H, GLA_DK, GLA_DV), F32)),
        scratch_shapes=[pltpu.VMEM((G, GLA_H, GLA_DK, GLA_DV), F32)],
        compiler_params=_params('parallel', 'arbitrary'), name='gla')(
            gqkv.reshape(bg, tg, 1024), gla.reshape(bg, tg, 256), gr.reshape(bg, tg, 512), s0, gnorm)
    return o.reshape(bg * tg, 512), sfin


def _conv_kernel(bch_ref, buf_ref, w_ref, o_ref, st_ref, prev, *, G, tt):
    @pl.when(pl.program_id(1) == 0)
    def _():
        prev[...] = buf_ref[...]

    w = w_ref[...]
    rowi = lax.broadcasted_iota(jnp.int32, (tt, CONV_D), 0)
    for g in range(G):
        bch = bch_ref[g]
        u = bch[:, CONV_D:2 * CONV_D] * bch[:, 2 * CONV_D:]
        p = prev[g]
        u1 = jnp.where(rowi == 0, p[1:2], pltpu.roll(u, 1, 0))
        u2 = jnp.where(rowi == 0, p[0:1], jnp.where(rowi == 1, p[1:2], pltpu.roll(u, 2, 0)))
        y = w[0:1] * u2 + w[1:2] * u1 + w[2:3] * u
        o_ref[g] = (bch[:, :CONV_D] * y).astype(o_ref.dtype)
        tail = u[tt - (CONV_W - 1):tt]
        prev[g] = tail
        st_ref[g] = tail


def _conv(bch, buf, conv_w, bg, tg, out_dtype):
    if tg >= 512:
        G, tt = 1, 512
    else:
        G, tt = min(bg, 16), tg
    assert bg % G == 0 and tg % tt == 0 and tt >= CONV_W - 1
    sblk = pl.BlockSpec((G, CONV_W - 1, CONV_D), lambda b, t: (b, 0, 0))
    o, st = pl.pallas_call(
        functools.partial(_conv_kernel, G=G, tt=tt), grid=(bg // G, tg // tt),
        in_specs=[pl.BlockSpec((G, tt, 3 * CONV_D), lambda b, t: (b, t, 0)), sblk,
                  _const_spec(conv_w.shape)],
        out_specs=(pl.BlockSpec((G, tt, CONV_D), lambda b, t: (b, t, 0)), sblk),
        out_shape=(jax.ShapeDtypeStruct((bg, tg, CONV_D), out_dtype),
                   jax.ShapeDtypeStruct((bg, CONV_W - 1, CONV_D), F32)),
        scratch_shapes=[pltpu.VMEM((G, CONV_W - 1, CONV_D), F32)],
        compiler_params=_params('parallel', 'arbitrary'), name='conv')(
            bch.reshape(bg, tg, 3 * CONV_D), buf, conv_w)
    return o.reshape(bg * tg, CONV_D), st


def _fox_prompt_kernel(*refs, tq, tk, hb, precise, q_off):
    if precise:
        q_ref, k_ref, v_ref, ql_ref, kl_ref, vl_ref, o_ref, m_scr, acc_scr, sa_scr, sb_scr = refs
    else:
        q_ref, k_ref, v_ref, o_ref, m_scr, acc_scr, sa_scr, sb_scr = refs
    qi = pl.program_id(2) + q_off
    nfull = (qi * tq) // tk
    row = lax.broadcasted_iota(jnp.int32, (tq, tk), 0)
    col = lax.broadcasted_iota(jnp.int32, (tq, tk), 1)
    m_scr[...] = jnp.full_like(m_scr, -jnp.inf)
    acc_scr[...] = jnp.zeros_like(acc_scr)

    def scores(s_ref, t):
        k0 = pl.multiple_of(t * tk, tk)
        for h in range(hb):
            ls = slice(h * AUG, (h + 1) * AUG)
            kt = k_ref[pl.ds(k0, tk), ls]
            s = _dot_nt(q_ref[:, ls], kt)
            if precise:
                s = s + _dot_nt(ql_ref[:, ls], kt) + _dot_nt(q_ref[:, ls], kl_ref[pl.ds(k0, tk), ls])
            s_ref[h] = s

    def update(s_ref, t, masked):
        k0 = pl.multiple_of(t * tk, tk)
        for h in range(hb):
            ls = slice(h * AUG, (h + 1) * AUG)
            s = s_ref[h]
            if masked:
                s = jnp.where(col + k0 <= row + qi * tq, s, -jnp.inf)
            m_prev = m_scr[h]
            m_new = jnp.maximum(m_prev, jnp.max(s, axis=-1, keepdims=True))
            alpha = jnp.exp(m_prev - m_new)
            p = jnp.exp(s - jnp.concatenate([m_new] * (tk // LANES), axis=1))
            pb = p.astype(BF16)
            vt = v_ref[pl.ds(k0, tk), ls]
            pv = _dot(pb, vt)
            if precise:
                pv = pv + _dot((p - pb.astype(F32)).astype(BF16), vt) + _dot(pb, vl_ref[pl.ds(k0, tk), ls])
            acc_scr[h] = alpha * acc_scr[h] + pv
            m_scr[h] = m_new

    npair = nfull // 2
    scores(sa_scr, 0)

    def body(i, carry):
        scores(sb_scr, 2 * i + 1)
        update(sa_scr, 2 * i, False)
        scores(sa_scr, 2 * i + 2)
        update(sb_scr, 2 * i + 1, False)
        return carry

    lax.fori_loop(0, npair, body, 0)

    @pl.when(nfull == 2 * npair)
    def _():
        update(sa_scr, nfull, True)

    @pl.when(nfull != 2 * npair)
    def _():
        scores(sb_scr, nfull)
        update(sa_scr, nfull - 1, False)
        update(sb_scr, nfull, True)
    lane = lax.broadcasted_iota(jnp.int32, (1, AUG), 1)
    outs = []
    for h in range(0, hb, 2):
        o = []
        for hh in (h, h + 1):
            acc = acc_scr[hh]
            o.append(acc / acc[:, FOX_HD:FOX_HD + 1])
        outs.append(jnp.where(lane < FOX_HD, o[0], pltpu.roll(o[1], FOX_HD, 1)))
    o_ref[...] = jnp.concatenate(outs, axis=1).astype(o_ref.dtype)


def _fox_prompt(qkv, bg, tg, n_run=None):
    tq = min(tg, 256)
    tk = min(tg, 512)
    hb = 4
    nq = tg // tq
    precise = len(qkv) == 6
    n_run = nq if n_run is None else n_run
    q_spec = pl.BlockSpec((tq, hb * AUG), lambda b, p, i: (b * nq + (nq - n_run) + i, p))
    kv_spec = pl.BlockSpec((tg, hb * AUG), lambda b, p, i: (b, p), pipeline_mode=pl.Buffered(1))
    specs = [q_spec, kv_spec, kv_spec]
    return pl.pallas_call(
        functools.partial(_fox_prompt_kernel, tq=tq, tk=tk, hb=hb, precise=precise, q_off=nq - n_run),
        grid=(bg, FOX_H // hb, n_run),
        in_specs=specs * (2 if precise else 1),
        out_specs=pl.BlockSpec((tq, hb * FOX_HD), lambda b, p, i: (b * n_run + i, p)),
        out_shape=jax.ShapeDtypeStruct((bg * n_run * tq, BRANCH_W), F32 if precise else BF16),
        scratch_shapes=[pltpu.VMEM((hb, tq, AUG), F32)] * 2 + [pltpu.VMEM((hb, tq, tk), F32)] * 2,
        compiler_params=_params('parallel', 'parallel', 'arbitrary'), name='fox_prompt')(*qkv)


def _fox_sample_kernel(pt_ref, q_ref, kn_ref, vn_ref, lfn_ref, *refs, P, tq, single):
    del pt_ref
    k_refs, v_refs, lf_refs = refs[:P], refs[P:2 * P], refs[2 * P:3 * P]
    o_ref, qbd, m_scr, l_scr, acc_scr, carry = refs[3 * P:]
    j = pl.program_id(1)
    nrow = tq * FOX_H
    hrow = lax.broadcasted_iota(jnp.int32, (FOX_H, BRANCH_W), 0)
    hcol = lax.broadcasted_iota(jnp.int32, (FOX_H, BRANCH_W), 1)
    hmask = (hcol // FOX_HD) == hrow

    @pl.when(j == 0)
    def _():
        q = q_ref[...]
        qbd[...] = jnp.concatenate(
            [jnp.where(hmask, jnp.broadcast_to(q[t:t + 1, :], (FOX_H, BRANCH_W)), 0.0) for t in range(tq)],
            axis=0).astype(BF16)
        m_scr[...] = jnp.full_like(m_scr, -jnp.inf)
        l_scr[...] = jnp.zeros_like(l_scr)
        acc_scr[...] = jnp.zeros_like(acc_scr)
        carry[...] = jnp.zeros_like(carry)

    def update(qks, lfs, own_last, pvs):
        ss = []
        cur = carry[...]
        for i, (qk, (pre, tot)) in enumerate(zip(qks, [_lane_prefix(lf) for lf in lfs])):
            sb = qk - jnp.concatenate([cur + pre] * tq, axis=0)
            if own_last and i == len(qks) - 1:
                key = lax.broadcasted_iota(jnp.int32, (nrow, PAGE_SIZE), 1)
                tok = lax.broadcasted_iota(jnp.int32, (nrow, PAGE_SIZE), 0) // FOX_H
                sb = jnp.where(key <= tok, sb, -jnp.inf)
            ss.append(sb)
            cur = cur + tot
        carry[...] = cur
        s = jnp.concatenate(ss, axis=1)
        m_prev = m_scr[...]
        m_new = jnp.maximum(m_prev, jnp.max(s, axis=-1, keepdims=True))
        alpha = jnp.exp(m_prev - m_new)
        p = jnp.exp(s - jnp.concatenate([m_new] * len(qks), axis=1))
        l_scr[...] = alpha * l_scr[...] + jnp.sum(p, axis=-1, keepdims=True)
        pb = p.astype(BF16)
        pv = None
        for i, f in enumerate(pvs):
            term = f(pb[:, i * PAGE_SIZE:(i + 1) * PAGE_SIZE])
            pv = term if pv is None else pv + term
        acc_scr[...] = jnp.concatenate([alpha] * (BRANCH_W // LANES), axis=1) * acc_scr[...] + pv
        m_scr[...] = m_new

    q_bd = qbd[...]
    kts = [k_refs[i][...].reshape(BRANCH_W, PAGE_SIZE).astype(BF16) for i in range(P)]
    vts = [v_refs[i][...].reshape(BRANCH_W, PAGE_SIZE).astype(BF16) for i in range(P)]
    page_qk = [_dot(q_bd, kt) for kt in kts]
    page_lf = [lf_refs[i][...] for i in range(P)]
    page_pv = [lambda p, vt=vt: _dot_nt(p, vt) for vt in vts]

    def own_block():
        zpad = jnp.zeros((PAGE_SIZE - tq, BRANCH_W), F32)
        kn = jnp.concatenate([kn_ref[...], zpad], axis=0).astype(BF16)
        vn = jnp.concatenate([vn_ref[...], zpad], axis=0).astype(BF16)
        return _dot_nt(q_bd, kn), lfn_ref[0], lambda p: _dot(p, vn)

    def finish():
        o = acc_scr[...] / jnp.concatenate([l_scr[...]] * (BRANCH_W // LANES), axis=1)
        rows = [jnp.sum(jnp.where(hmask, o[t * FOX_H:(t + 1) * FOX_H], 0.0), axis=0, keepdims=True)
                for t in range(tq)]
        o_ref[...] = jnp.concatenate(rows, axis=0)

    if single:
        qk_n, lf_n, pv_n = own_block()
        update(page_qk + [qk_n], page_lf + [lf_n], True, page_pv + [pv_n])
        finish()
    else:
        update(page_qk, page_lf, False, page_pv)

        @pl.when(j == pl.num_programs(1) - 1)
        def _():
            qk_n, lf_n, pv_n = own_block()
            update([qk_n], [lf_n], True, [pv_n])
            finish()


def _fox_sample(fq, fk32, fv32, lft, layer, k_cache, v_cache, lf_cache, page_table, db, tq):
    n_pages = page_table.shape[1]
    P = next(p for p in (16, 8, 4, 2, 1) if n_pages % p == 0)
    assert tq == SUBLANES
    kt = jnp.transpose(k_cache, (0, 1, 3, 4, 2))
    vt = jnp.transpose(v_cache, (0, 1, 3, 4, 2))
    lfp = jnp.swapaxes(lf_cache, 2, 3)
    lfn = jnp.swapaxes(lft.reshape(FOX_H, db, tq), 0, 1)
    lfn = jnp.pad(lfn, ((0, 0), (0, 0), (0, PAGE_SIZE - tq)))
    tok = lambda: pl.BlockSpec((tq, BRANCH_W), lambda b, j, pt: (b, 0))
    page = lambda i: pl.BlockSpec((None, None, FOX_H, FOX_HD, PAGE_SIZE),
                                  lambda b, j, pt, i=i: (layer, pt[b, j * P + i], 0, 0, 0))
    lfpage = lambda i: pl.BlockSpec((None, None, FOX_H, PAGE_SIZE),
                                    lambda b, j, pt, i=i: (layer, pt[b, j * P + i], 0, 0))
    nrow = tq * FOX_H
    return pl.pallas_call(
        functools.partial(_fox_sample_kernel, P=P, tq=tq, single=n_pages == P),
        grid_spec=pltpu.PrefetchScalarGridSpec(
            num_scalar_prefetch=1, grid=(db, n_pages // P),
            in_specs=[tok(), tok(), tok(), pl.BlockSpec((1, FOX_H, PAGE_SIZE), lambda b, j, pt: (b, 0, 0))]
                     + [page(i) for i in range(P)] + [page(i) for i in range(P)]
                     + [lfpage(i) for i in range(P)],
            out_specs=tok(),
            scratch_shapes=[pltpu.VMEM((nrow, BRANCH_W), BF16), pltpu.VMEM((nrow, LANES), F32),
                            pltpu.VMEM((nrow, LANES), F32), pltpu.VMEM((nrow, BRANCH_W), F32),
                            pltpu.VMEM((FOX_H, LANES), F32)]),
        out_shape=jax.ShapeDtypeStruct((db * tq, BRANCH_W), F32),
        compiler_params=_params('parallel', 'arbitrary'), name='fox_sample')(
            page_table, fq, fk32, fv32, lfn, *([kt] * P), *([vt] * P), *([lfp] * P))


def _merge_kernel(x_ref, og_ref, of_ref, oc_ref, wg_ref, wb_ref, wo_ref, g_ref, b_ref, y_ref, *, precise):
    x = x_ref[...]
    xs = _pieces(x, precise)
    m = None
    for jb, o_ref in enumerate((og_ref, of_ref, oc_ref)):
        gj = _sigmoid(_mm(xs, wg_ref.at[:, :, jb * D_MODEL:(jb + 1) * D_MODEL]))
        term = gj * _mm(_pieces(o_ref[...].astype(F32), precise), wb_ref.at[:, jb])
        m = term if m is None else m + term
    y = DEEPNORM_ALPHA * x + _mm(_pieces(m, precise), wo_ref)
    y_ref[...] = _layernorm(y, g_ref[...], b_ref[...])


def _merge(x, o_gla, o_fox, o_conv, w, precise):
    rows = x.shape[0]
    tm = _row_tile(rows, 256)
    row = lambda n: pl.BlockSpec((tm, n), lambda i: (i, 0))
    consts = [w['wg'], w['wb'], w['wo'], w['ln1_g'], w['ln1_b']]
    return pl.pallas_call(
        functools.partial(_merge_kernel, precise=precise), grid=(rows // tm,),
        in_specs=[row(D_MODEL), row(BRANCH_W), row(BRANCH_W), row(BRANCH_W)]
                 + [_const_spec(a.shape) for a in consts],
        out_specs=row(D_MODEL), out_shape=jax.ShapeDtypeStruct((rows, D_MODEL), F32),
        compiler_params=_params('parallel'), name='merge')(x, o_gla, o_fox, o_conv, *consts)


def _route(probs):
    rows = [probs[i:i + 1, :] for i in range(N_EXPERTS)]
    gscore = []
    for g in range(N_GROUPS):
        r0, r1, r2, r3 = rows[EXP_PER_GROUP * g:EXP_PER_GROUP * (g + 1)]
        a, b = jnp.maximum(r0, r1), jnp.minimum(r0, r1)
        c, d = jnp.maximum(r2, r3), jnp.minimum(r2, r3)
        gscore.append(jnp.maximum(a, c) + jnp.maximum(jnp.minimum(a, c), jnp.maximum(b, d)))
    best, gsel = gscore[0], jnp.zeros_like(gscore[0], dtype=jnp.int32)
    for g in range(1, N_GROUPS):
        upd = gscore[g] > best
        best = jnp.where(upd, gscore[g], best)
        gsel = jnp.where(upd, g, gsel)
    v = []
    for i in range(EXP_PER_GROUP):
        vi = rows[(N_GROUPS - 1) * EXP_PER_GROUP + i]
        for g in range(N_GROUPS - 2, -1, -1):
            vi = jnp.where(gsel == g, rows[g * EXP_PER_GROUP + i], vi)
        v.append(vi)
    v1, i1 = v[0], jnp.zeros_like(gsel)
    for i in range(1, EXP_PER_GROUP):
        upd = v[i] > v1
        v1 = jnp.where(upd, v[i], v1)
        i1 = jnp.where(upd, i, i1)
    v2, i2 = jnp.full_like(v1, -1.0), jnp.full_like(gsel, -1)
    for i in range(EXP_PER_GROUP):
        cand = jnp.where(i1 == i, -1.0, v[i])
        upd = cand > v2
        v2 = jnp.where(upd, cand, v2)
        i2 = jnp.where(upd, i, i2)
    den = v1 + v2
    w1, w2 = v1 / den, v2 / den
    out = []
    for g in range(N_GROUPS):
        for i in range(EXP_PER_GROUP):
            wi = jnp.where(i1 == i, w1, jnp.where(i2 == i, w2, 0.0))
            out.append(jnp.where(gsel == g, wi, 0.0))
    return out


def _moe_kernel(x_ref, wrt_ref, br_ref, wg_ref, wu_ref, wd_ref, g_ref, b_ref, y_ref):
    x = x_ref[...]
    tm = x.shape[0]
    xb = x.astype(BF16)
    x_lo = (x - xb.astype(F32)).astype(BF16)
    wrt = wrt_ref[...]
    w_hi = wrt.astype(BF16)
    w_lo = (wrt - w_hi.astype(F32)).astype(BF16)
    logits = _dot_nt(w_hi, xb) + _dot_nt(w_hi, x_lo) + _dot_nt(w_lo, xb) + br_ref[...]
    e = jnp.exp(logits - jnp.max(logits, axis=0, keepdims=True))
    probs = e / jnp.sum(e, axis=0, keepdims=True)
    gate_rows = _route(probs)
    gate_t = jnp.concatenate(gate_rows + [jnp.zeros((LANES - N_EXPERTS, tm), F32)], axis=0)
    gate = gate_t.T
    acc = jnp.zeros((tm, D_MODEL), F32)
    for ex in range(N_EXPERTS):
        hg = _dot(xb, wg_ref[ex])
        hu = _dot(xb, wu_ref[ex])
        h = (hg * _sigmoid(hg)) * hu * gate[:, ex:ex + 1]
        acc = acc + _dot(h.astype(BF16), wd_ref[ex])
    y = DEEPNORM_ALPHA * x + acc
    y_ref[...] = _layernorm(y, g_ref[...], b_ref[...])


def _moe(x, w):
    rows = x.shape[0]
    tm = _row_tile(rows, 512)
    row = pl.BlockSpec((tm, D_MODEL), lambda i: (i, 0))
    consts = [w['wrt'], w['br'], w['weg'], w['weu'], w['wed'], w['ln2_g'], w['ln2_b']]
    return pl.pallas_call(
        _moe_kernel, grid=(rows // tm,),
        in_specs=[row] + [_const_spec(a.shape) for a in consts],
        out_specs=row, out_shape=jax.ShapeDtypeStruct((rows, D_MODEL), F32),
        compiler_params=_params('parallel'), name='moe')(x, *consts)


def _prep_weights(l, precise, w_in, w_gla_a2, b_gla_a, gla_norm_g, b_fox_f, conv_w, w_branch, w_out, ln1_g, ln1_b,
                  w_router, b_router, w_e_gate, w_e_up, w_e_down, ln2_g, ln2_b):
    wi = w_in[l]
    off = {}
    o = 0
    for name, n in (('gla_q', 256), ('gla_k', 256), ('gla_v', 512), ('gla_a', GLA_LR), ('gla_r', 512),
                    ('fox_q', 512), ('fox_k', 512), ('fox_v', 512), ('fox_f', FOX_H),
                    ('conv', 3 * CONV_D), ('gate', N_BRANCH * D_MODEL)):
        off[name] = (o, o + n)
        o += n
    col = lambda name: wi[:, off[name][0]:off[name][1]]
    padc = lambda a, n: jnp.pad(a, ((0, 0), (0, n - a.shape[1])))
    row2 = lambda a: a.reshape(1, -1)

    def bf(a):
        hi = a.astype(BF16)
        return jnp.stack([hi, (a - hi.astype(F32)).astype(BF16)]) if precise else hi[None]

    ff = col('fox_f')
    return {
        'wqkv': bf(jnp.concatenate([col('gla_q') * (GLA_DK ** -0.5), col('gla_k'), col('gla_v')], axis=1)),
        'wa': bf(padc(col('gla_a'), LANES)),
        'wa2': bf(jnp.pad(w_gla_a2[l], ((0, LANES - GLA_LR), (0, 0)))),
        'ba': row2(b_gla_a[l]),
        'wr': bf(col('gla_r')),
        'wfq': bf((col('fox_q') * (FOX_HD ** -0.5))),
        'wfk': bf(col('fox_k')),
        'wfv': bf(col('fox_v')),
        'wfkt': bf(col('fox_k').T),
        'wfvt': bf(col('fox_v').T),
        'wff': bf(padc(ff, LANES)),
        'bff': padc(row2(b_fox_f[l]), LANES),
        'wfft': bf(jnp.pad(ff.T, ((0, 2 * SUBLANES - FOX_H), (0, 0)))),
        'bfft': b_fox_f[l].reshape(FOX_H, 1),
        'pad_ones': (jnp.arange(BRANCH_W)[None, :] % FOX_HD < jnp.arange(1, SUBLANES + 1)[:, None]).astype(F32),
        'wc': bf(col('conv')),
        'wg': bf(col('gate')),
        'gnorm': row2(gla_norm_g[l]),
        'conv_w': conv_w[l],
        'wb': bf(w_branch[l]),
        'wo': bf(w_out[l]),
        'ln1_g': row2(ln1_g[l]), 'ln1_b': row2(ln1_b[l]),
        'wrt': w_router.T, 'br': b_router.reshape(N_EXPERTS, 1),
        'weg': w_e_gate[l].astype(BF16), 'weu': w_e_up[l].astype(BF16), 'wed': w_e_down[l].astype(BF16),
        'ln2_g': row2(ln2_g[l]), 'ln2_b': row2(ln2_b[l]),
    }


TAIL = 256
SPLICE = 128


def _layer(x, w, bg, tg, gla_s0, conv_buf, pools, act_dtype, precise):
    prompt = pools is None
    precise = precise and prompt and tg >= 2 * TAIL
    gqkv, gla, gr, fk, fv, lf, *fox_in = _proj1(x, w, prompt, tg, precise)
    bch = _proj2(x, w, False)
    o_gla, gla_state = _gla(gqkv, gla, gr, gla_s0, w['gnorm'], bg, tg, F32 if precise else act_dtype, precise)
    if prompt:
        o_fox = _fox_prompt(fox_in[:3], bg, tg)
        caches = (fk.reshape(bg, FOX_H, FOX_HD, tg), fv.reshape(bg, FOX_H, FOX_HD, tg), lf)
    else:
        o_fox = _fox_sample(fox_in[0], fk, fv, fox_in[1], *pools, bg, tg)
        caches = (fk.reshape(bg, tg, FOX_H, FOX_HD), fv.reshape(bg, tg, FOX_H, FOX_HD), lf.reshape(bg, tg, FOX_H))
    o_conv, conv_state = _conv(bch, conv_buf, w['conv_w'], bg, tg, act_dtype)
    y = _moe(_merge(x, o_gla, o_fox, o_conv, w, False), w)
    if precise:
        n_t = TAIL // min(tg, 256)
        tail = lambda a: a.reshape(bg, tg, a.shape[-1])[:, tg - TAIL:].reshape(bg * TAIL, a.shape[-1])
        xt = tail(x)
        bch_t = _proj2(xt, w, True)
        o_fox_t = _fox_prompt(fox_in, bg, tg, n_run=n_t)
        o_conv_t, _ = _conv(bch_t, jnp.zeros_like(conv_buf), w['conv_w'], bg, TAIL, F32)
        y_t = _moe(_merge(xt, tail(o_gla), o_fox_t, o_conv_t, w, True), w)
        y = y.reshape(bg, tg, D_MODEL).at[:, tg - SPLICE:].set(
            y_t.reshape(bg, TAIL, D_MODEL)[:, TAIL - SPLICE:]).reshape(bg * tg, D_MODEL)
    return y, caches + (gla_state, conv_state)


def kernel(x_prompt, x_sample, cache_fox_k, cache_fox_v, cache_fox_logf, state_gla, state_conv, page_table, w_in, w_gla_a2, b_gla_a, gla_norm_g, b_fox_f, conv_w, w_branch, w_out, ln1_g, ln1_b, w_router, b_router, w_e_gate, w_e_up, w_e_down, ln2_g, ln2_b):
    B, T, _ = x_prompt.shape
    DB, TS, _ = x_sample.shape
    xp = x_prompt.reshape(B * T, D_MODEL)
    xs = x_sample.reshape(DB * TS, D_MODEL)
    depth = w_in.shape[0]
    st_p, st_s = [], []
    for l in range(depth):
        precise = l < depth - 1
        w = _prep_weights(l, precise, w_in, w_gla_a2, b_gla_a, gla_norm_g, b_fox_f, conv_w, w_branch, w_out,
                          ln1_g, ln1_b, w_router, b_router, w_e_gate, w_e_up, w_e_down, ln2_g, ln2_b)
        xp, sp = _layer(xp, w, B, T, jnp.zeros((B, GLA_H, GLA_DK, GLA_DV), F32),
                        jnp.zeros((B, CONV_W - 1, CONV_D), F32), None, BF16, precise)
        xs, ss = _layer(xs, w, DB, TS, state_gla[l], state_conv[l],
                        (l, cache_fox_k, cache_fox_v, cache_fox_logf, page_table), F32, False)
        st_p.append(sp)
        st_s.append(ss)
    stack = lambda sts, i: jnp.stack([s[i] for s in sts])
    return (xp.reshape(B, T, D_MODEL), xs.reshape(DB, TS, D_MODEL),
            jnp.transpose(stack(st_p, 0), (0, 1, 4, 2, 3)), jnp.transpose(stack(st_p, 1), (0, 1, 4, 2, 3)),
            jnp.swapaxes(stack(st_p, 2), 2, 3), stack(st_p, 3), stack(st_p, 4),
            stack(st_s, 0), stack(st_s, 1), stack(st_s, 2), stack(st_s, 3), stack(st_s, 4))
```

```python
import functools

import jax
import jax.numpy as jnp
from jax import lax
from jax.experimental import pallas as pl
from jax.experimental.pallas import tpu as pltpu

F32 = jnp.float32
BF16 = jnp.bfloat16

D_MODEL = 1024
DEPTH = 2
PAGE_SIZE = 128
BRANCH_W = D_MODEL // 2
GLA_H = 4
GLA_DK = BRANCH_W // (2 * GLA_H)
GLA_DV = BRANCH_W // GLA_H
GLA_LR = 16
GLA_TAU = 16.0
GLA_CHUNK = 64
FOX_HD = 64
FOX_H = BRANCH_W // FOX_HD
CONV_D = BRANCH_W
CONV_W = 3
N_BRANCH = 3
N_EXPERTS = 16
N_GROUPS = 4
EXP_PER_GROUP = N_EXPERTS // N_GROUPS
D_EXPERT = D_MODEL // 4
LN_EPS = 1e-5
DEEPNORM_ALPHA = (2 * DEPTH) ** 0.25

LANES = 128
SUBLANES = 8
VMEM_LIMIT = 56 * 1024 * 1024

_NT = (((1,), (1,)), ((), ()))
_TN = (((0,), (0,)), ((), ()))


def _dot(a, b):
    return jnp.dot(a, b, preferred_element_type=F32)


def _dot_nt(a, b):
    return lax.dot_general(a, b, _NT, preferred_element_type=F32)


def _dot_tn(a, b):
    return lax.dot_general(a, b, _TN, preferred_element_type=F32)


def _split3(x):
    hi = x.astype(BF16)
    r1 = x - hi.astype(F32)
    mid = r1.astype(BF16)
    lo = (r1 - mid.astype(F32)).astype(BF16)
    return hi, mid, lo


def _pieces(x, precise):
    hi = x.astype(BF16)
    if not precise:
        return (hi,)
    return hi, (x - hi.astype(F32)).astype(BF16)


def _mm(xs, w_ref):
    y = _dot(xs[0], w_ref[0])
    if len(xs) == 2:
        y = y + _dot(xs[1], w_ref[0]) + _dot(xs[0], w_ref[1])
    return y


def _mm_nt(w_ref, xs):
    y = _dot_nt(w_ref[0], xs[0])
    if len(xs) == 2:
        y = y + _dot_nt(w_ref[0], xs[1]) + _dot_nt(w_ref[1], xs[0])
    return y


def _pdot(dot, a, b, precise):
    ap, bp = _pieces(a, precise), _pieces(b, precise)
    y = dot(ap[0], bp[0])
    if precise:
        y = y + dot(ap[1], bp[0]) + dot(ap[0], bp[1])
    return y


def _log_sigmoid(x):
    return jnp.minimum(x, 0.0) - jnp.log1p(jnp.exp(-jnp.abs(x)))


def _sigmoid(x):
    return 1.0 / (1.0 + jnp.exp(-x))


def _layernorm(y, g, b):
    mu = jnp.mean(y, axis=-1, keepdims=True)
    d = y - mu
    var = jnp.mean(d * d, axis=-1, keepdims=True)
    return d * lax.rsqrt(var + LN_EPS) * g + b


def _lane_prefix(x):
    r = lax.broadcasted_iota(jnp.int32, (LANES, 2 * LANES), 0)
    c = lax.broadcasted_iota(jnp.int32, (LANES, 2 * LANES), 1)
    u = jnp.where(r <= c, 1.0, 0.0).astype(BF16)
    hi, mid, lo = _split3(x)
    y = _dot(hi, u) + _dot(mid, u) + _dot(lo, u)
    return y[:, :LANES], y[:, LANES:]


def _const_spec(shape):
    nd = len(shape)
    return pl.BlockSpec(shape, lambda *_: (0,) * nd, pipeline_mode=pl.Buffered(1))


def _params(*sem):
    return pltpu.CompilerParams(dimension_semantics=sem, vmem_limit_bytes=VMEM_LIMIT)


def _row_tile(rows, want):
    t = min(rows, want)
    assert rows % t == 0
    return t


AUG = 2 * FOX_HD
AUG_W = FOX_H * AUG


def _mm_rows(tail, xs, w_ref, store):
    if tail is None or len(xs) == 1:
        store(_mm(xs, w_ref))
        return

    @pl.when(tail)
    def _():
        store(_mm(xs, w_ref))

    @pl.when(jnp.logical_not(tail))
    def _():
        store(_mm(xs[:1], w_ref))


def _gla_cols(xs, tail, wqkv_ref, wa_ref, wa2_ref, ba_ref, wr_ref, gqkv_ref, gla_ref, gr_ref):
    nq = GLA_H * GLA_DK

    def put_q(y):
        gqkv_ref[:, :nq] = y

    def put_r(r):
        gr_ref[...] = r * _sigmoid(r)

    _mm_rows(tail, xs, wqkv_ref.at[:, :, :nq], put_q)
    gqkv_ref[:, nq:] = _mm(xs, wqkv_ref.at[:, :, nq:])
    a_pre = _mm(_pieces(_mm(xs, wa_ref), len(xs) == 2), wa2_ref) + ba_ref[...]
    gla_ref[...] = _log_sigmoid(a_pre) * (1.0 / GLA_TAU)
    _mm_rows(tail, xs, wr_ref, put_r)


def _proj1_prompt_kernel(x_ref, wqkv_ref, wa_ref, wa2_ref, ba_ref, wr_ref, wfk_ref, wfv_ref, wff_ref, bff_ref,
                         wfft_ref, bfft_ref, wqa_ref, wka_ref, wva_ref, pad_ref,
                         gqkv_ref, gla_ref, gr_ref, fk_ref, fv_ref, lf_ref, qa_ref, ka_ref, va_ref,
                         *rest, tiles_per_seq, tail_tiles, precise):
    tm = x_ref.shape[0]
    lo_refs, carry = rest[:-1], rest[-1]

    @pl.when(pl.program_id(0) % tiles_per_seq == 0)
    def _():
        carry[...] = jnp.zeros_like(carry)

    xs = _pieces(x_ref[...], precise)
    tail = (pl.program_id(0) % tiles_per_seq >= tiles_per_seq - tail_tiles) if precise else None
    _gla_cols(xs, tail, wqkv_ref, wa_ref, wa2_ref, ba_ref, wr_ref, gqkv_ref, gla_ref, gr_ref)
    fk_ref[...] = _mm_nt(wfk_ref, xs[:1])
    fv_ref[...] = _mm_nt(wfv_ref, xs[:1])
    lf_ref[...] = _log_sigmoid(_mm_nt(wfft_ref, xs[:1])[:FOX_H] + bfft_ref[...])
    lf = _log_sigmoid(_mm(xs, wff_ref) + bff_ref[...])
    ri = lax.broadcasted_iota(jnp.int32, (tm, tm), 0)
    ci = lax.broadcasted_iota(jnp.int32, (tm, tm), 1)
    lmat = jnp.where(ci <= ri, 1.0, 0.0).astype(BF16)
    hi, mid, lo = _split3(lf)
    c = _dot(lmat, hi) + _dot(lmat, mid) + _dot(lmat, lo) + carry[0:1, :]
    carry[...] = jnp.broadcast_to(c[tm - 1:tm, :], carry.shape)
    pr = lax.broadcasted_iota(jnp.int32, (LANES, BRANCH_W), 0)
    pc = lax.broadcasted_iota(jnp.int32, (LANES, BRANCH_W), 1)
    kpad = None
    for j, part in enumerate(_split3(-c)):
        term = _dot(part, jnp.where(pc == pr * FOX_HD + j, 1.0, 0.0).astype(BF16))
        kpad = term if kpad is None else kpad + term
    ones = lambda n: jnp.broadcast_to(pad_ref[n - 1:n, :], (tm, BRANCH_W))

    def put(i, val, pad, hi_ref):
        parts = []
        for h in range(FOX_H):
            sl = slice(h * FOX_HD, (h + 1) * FOX_HD)
            parts += [val[:, sl], pad[:, sl]]
        aug = jnp.concatenate(parts, axis=1)
        hi = aug.astype(BF16)
        hi_ref[...] = hi
        if precise:
            lo_refs[i][...] = (aug - hi.astype(F32)).astype(BF16)

    put(1, _mm(xs, wka_ref), kpad, ka_ref)
    put(2, _mm(xs, wva_ref), ones(1), va_ref)
    _mm_rows(tail, xs, wqa_ref, lambda q: put(0, q, ones(3), qa_ref))


def _proj1_sample_kernel(x_ref, wqkv_ref, wa_ref, wa2_ref, ba_ref, wr_ref, wfk_ref, wfv_ref, wff_ref, bff_ref,
                         wfft_ref, bfft_ref, wfq_ref,
                         gqkv_ref, gla_ref, gr_ref, fk_ref, fv_ref, lf_ref, fq_ref, lft_ref):
    xs = _pieces(x_ref[...], False)
    _gla_cols(xs, None, wqkv_ref, wa_ref, wa2_ref, ba_ref, wr_ref, gqkv_ref, gla_ref, gr_ref)
    fk_ref[...] = _mm(xs, wfk_ref)
    fv_ref[...] = _mm(xs, wfv_ref)
    lf_ref[...] = _log_sigmoid(_mm(xs, wff_ref) + bff_ref[...])[:, :FOX_H]
    fq_ref[...] = _mm(xs, wfq_ref)
    lft_ref[...] = _log_sigmoid(_mm_nt(wfft_ref, xs)[:FOX_H] + bfft_ref[...])


def _proj1(x, w, prompt, tg, precise):
    rows = x.shape[0]
    tm = _row_tile(rows, 256)
    row = lambda n: pl.BlockSpec((tm, n), lambda i: (i, 0))
    common = ('wqkv', 'wa', 'wa2', 'ba', 'wr', 'wfkt' if prompt else 'wfk', 'wfvt' if prompt else 'wfv',
              'wff', 'bff', 'wfft', 'bfft')
    sds = lambda n, dt: jax.ShapeDtypeStruct((rows, n), dt)
    out_shape = [sds(1024, F32), sds(256, F32), sds(512, F32)]
    out_specs = [row(1024), row(256), row(512)]
    if prompt:
        assert tg % tm == 0
        tps = tg // tm
        names = common + ('wfq', 'wfk', 'wfv', 'pad_ones')
        kern = functools.partial(_proj1_prompt_kernel, tiles_per_seq=tps, tail_tiles=-(-TAIL // tm),
                                 precise=precise)
        tr = lambda n: pl.BlockSpec((None, n, tm), lambda i: (i // tps, 0, i % tps))
        trs = lambda n: jax.ShapeDtypeStruct((rows // tg, n, tg), F32)
        n_aug = 6 if precise else 3
        out_shape += [trs(512), trs(512), trs(FOX_H)] + [sds(AUG_W, BF16)] * n_aug
        out_specs += [tr(512), tr(512), tr(FOX_H)] + [row(AUG_W)] * n_aug
        scratch = [pltpu.VMEM((SUBLANES, LANES), F32)]
    else:
        names = common + ('wfq',)
        kern = _proj1_sample_kernel
        out_shape += [sds(512, F32), sds(512, F32), sds(FOX_H, F32), sds(512, F32),
                      jax.ShapeDtypeStruct((FOX_H, rows), F32)]
        out_specs += [row(512), row(512), row(FOX_H), row(512), pl.BlockSpec((FOX_H, tm), lambda i: (0, i))]
        scratch = []
    ws = [w[n] for n in names]
    return pl.pallas_call(
        kern, grid=(rows // tm,),
        in_specs=[row(D_MODEL)] + [_const_spec(a.shape) for a in ws],
        out_specs=out_specs, out_shape=out_shape, scratch_shapes=scratch,
        compiler_params=_params('arbitrary'), name='proj1_prompt' if prompt else 'proj1_sample')(x, *ws)


def _proj2_kernel(x_ref, wc_ref, bch_ref, *, precise):
    bch_ref[...] = _mm(_pieces(x_ref[...], precise), wc_ref)


def _proj2(x, w, precise):
    rows = x.shape[0]
    tm = _row_tile(rows, 512)
    row = lambda n: pl.BlockSpec((tm, n), lambda i: (i, 0))
    return pl.pallas_call(
        functools.partial(_proj2_kernel, precise=precise), grid=(rows // tm,),
        in_specs=[row(D_MODEL), _const_spec(w['wc'].shape)],
        out_specs=row(3 * CONV_D), out_shape=jax.ShapeDtypeStruct((rows, 3 * CONV_D), F32),
        compiler_params=_params('parallel'), name='proj2')(x, w['wc'])


def _gla_kernel(qkv_ref, la_ref, r_ref, s0_ref, g_ref, o_ref, sfin_ref, state, *, G, cr, cp, nchunk, precise):
    ti = pl.program_id(1)

    @pl.when(ti == 0)
    def _():
        state[...] = s0_ref[...]

    ri = lax.broadcasted_iota(jnp.int32, (cp, cp), 0)
    ci = lax.broadcasted_iota(jnp.int32, (cp, cp), 1)
    tril = ci <= ri
    lmat = jnp.where(tril, 1.0, 0.0).astype(BF16)
    ones = jnp.ones((cp, GLA_DV), BF16)
    gnorm = g_ref[...]

    def pad(a):
        if cp == cr:
            return a
        return jnp.concatenate([a, jnp.zeros((cp - cr, a.shape[1]), a.dtype)], axis=0)

    def chunk(c, carry):
        r0 = pl.multiple_of(c * cr, cr)
        for g in range(G):
            qkv = pad(qkv_ref[g, pl.ds(r0, cr), :])
            la = pad(la_ref[g, pl.ds(r0, cr), :])
            q = qkv[:, :GLA_H * GLA_DK]
            k = qkv[:, GLA_H * GLA_DK:2 * GLA_H * GLA_DK]
            v = qkv[:, 2 * GLA_H * GLA_DK:]
            la3 = _split3(la)
            b = _dot(lmat, la3[0]) + _dot(lmat, la3[1]) + _dot(lmat, la3[2])
            b_last = b[cp - 1:cp, :]
            q_in = q * jnp.exp(b)
            k_in = k * jnp.exp(-b)
            k_end = k * jnp.exp(b_last - b)
            outs = []
            for h in range(GLA_H):
                ks = slice(h * GLA_DK, (h + 1) * GLA_DK)
                vs = slice(h * GLA_DV, (h + 1) * GLA_DV)
                s_prev = state[g, h]
                att = jnp.where(tril, _pdot(_dot_nt, q_in[:, ks], k_in[:, ks], precise), 0.0)
                o = _pdot(_dot, att, v[:, vs], precise) + _pdot(_dot, q_in[:, ks], s_prev, precise)
                dlog = (_dot_tn(la3[0][:, ks], ones) + _dot_tn(la3[1][:, ks], ones)
                        + _dot_tn(la3[2][:, ks], ones))
                state[g, h] = jnp.exp(dlog) * s_prev + _pdot(_dot_tn, k_end[:, ks], v[:, vs], precise)
                ms = jnp.mean(o * o, axis=-1, keepdims=True)
                outs.append(o * lax.rsqrt(ms + LN_EPS) * gnorm[:, vs])
            o_all = jnp.concatenate(outs, axis=1)[:cr] * r_ref[g, pl.ds(r0, cr), :]
            o_ref[g, pl.ds(r0, cr), :] = o_all.astype(o_ref.dtype)
        return carry

    lax.fori_loop(0, nchunk, chunk, 0, unroll=4 if nchunk % 4 == 0 else 1)

    @pl.when(ti == pl.num_programs(1) - 1)
    def _():
        sfin_ref[...] = state[...]


def _gla(gqkv, gla, gr, s0, gnorm, bg, tg, out_dtype, precise):
    if tg % GLA_CHUNK == 0:
        G, cr, cp, tt = 2 if bg % 2 == 0 else 1, GLA_CHUNK, GLA_CHUNK, min(tg, 512)
    else:
        assert tg % SUBLANES == 0 and tg <= 2 * SUBLANES
        G, cr, cp, tt = SUBLANES, tg, 2 * SUBLANES, tg
    assert bg % G == 0 and tg % tt == 0
    blk = lambda n: pl.BlockSpec((G, tt, n), lambda b, t: (b, t, 0))
    sblk = pl.BlockSpec((G, GLA_H, GLA_DK, GLA_DV), lambda b, t: (b, 0, 0, 0))
    kern = functools.partial(_gla_kernel, G=G, cr=cr, cp=cp, nchunk=tt // cr, precise=precise)
    o, sfin = pl.pallas_call(
        kern, grid=(bg // G, tg // tt),
        in_specs=[blk(1024), blk(256), blk(512), sblk, _const_spec(gnorm.shape)],
        out_specs=(blk(512), sblk),
        out_shape=(jax.ShapeDtypeStruct((bg, tg, 512), out_dtype),
                   jax.ShapeDtypeStruct((bg, GLA_H, GLA_DK, GLA_DV), F32)),
        scratch_shapes=[pltpu.VMEM((G, GLA_H, GLA_DK, GLA_DV), F32)],
        compiler_params=_params('parallel', 'arbitrary'), name='gla')(
            gqkv.reshape(bg, tg, 1024), gla.reshape(bg, tg, 256), gr.reshape(bg, tg, 512), s0, gnorm)
    return o.reshape(bg * tg, 512), sfin


def _conv_kernel(bch_ref, buf_ref, w_ref, o_ref, st_ref, prev, *, G, tt):
    @pl.when(pl.program_id(1) == 0)
    def _():
        prev[...] = buf_ref[...]

    w = w_ref[...]
    rowi = lax.broadcasted_iota(jnp.int32, (tt, CONV_D), 0)
    for g in range(G):
        bch = bch_ref[g]
        u = bch[:, CONV_D:2 * CONV_D] * bch[:, 2 * CONV_D:]
        p = prev[g]
        u1 = jnp.where(rowi == 0, p[1:2], pltpu.roll(u, 1, 0))
        u2 = jnp.where(rowi == 0, p[0:1], jnp.where(rowi == 1, p[1:2], pltpu.roll(u, 2, 0)))
        y = w[0:1] * u2 + w[1:2] * u1 + w[2:3] * u
        o_ref[g] = (bch[:, :CONV_D] * y).astype(o_ref.dtype)
        tail = u[tt - (CONV_W - 1):tt]
        prev[g] = tail
        st_ref[g] = tail


def _conv(bch, buf, conv_w, bg, tg, out_dtype):
    if tg >= 512:
        G, tt = 1, 512
    else:
        G, tt = min(bg, 16), tg
    assert bg % G == 0 and tg % tt == 0 and tt >= CONV_W - 1
    sblk = pl.BlockSpec((G, CONV_W - 1, CONV_D), lambda b, t: (b, 0, 0))
    o, st = pl.pallas_call(
        functools.partial(_conv_kernel, G=G, tt=tt), grid=(bg // G, tg // tt),
        in_specs=[pl.BlockSpec((G, tt, 3 * CONV_D), lambda b, t: (b, t, 0)), sblk,
                  _const_spec(conv_w.shape)],
        out_specs=(pl.BlockSpec((G, tt, CONV_D), lambda b, t: (b, t, 0)), sblk),
        out_shape=(jax.ShapeDtypeStruct((bg, tg, CONV_D), out_dtype),
                   jax.ShapeDtypeStruct((bg, CONV_W - 1, CONV_D), F32)),
        scratch_shapes=[pltpu.VMEM((G, CONV_W - 1, CONV_D), F32)],
        compiler_params=_params('parallel', 'arbitrary'), name='conv')(
            bch.reshape(bg, tg, 3 * CONV_D), buf, conv_w)
    return o.reshape(bg * tg, CONV_D), st


def _fox_prompt_kernel(*refs, tq, tk, hb, precise, q_off):
    if precise:
        q_ref, k_ref, v_ref, ql_ref, kl_ref, vl_ref, o_ref, m_scr, acc_scr, sa_scr, sb_scr = refs
    else:
        q_ref, k_ref, v_ref, o_ref, m_scr, acc_scr, sa_scr, sb_scr = refs
    qi = pl.program_id(2) + q_off
    nfull = (qi * tq) // tk
    row = lax.broadcasted_iota(jnp.int32, (tq, tk), 0)
    col = lax.broadcasted_iota(jnp.int32, (tq, tk), 1)
    m_scr[...] = jnp.full_like(m_scr, -jnp.inf)
    acc_scr[...] = jnp.zeros_like(acc_scr)

    def scores(s_ref, t):
        k0 = pl.multiple_of(t * tk, tk)
        for h in range(hb):
            ls = slice(h * AUG, (h + 1) * AUG)
            kt = k_ref[pl.ds(k0, tk), ls]
            s = _dot_nt(q_ref[:, ls], kt)
            if precise:
                s = s + _dot_nt(ql_ref[:, ls], kt) + _dot_nt(q_ref[:, ls], kl_ref[pl.ds(k0, tk), ls])
            s_ref[h] = s

    def update(s_ref, t, masked):
        k0 = pl.multiple_of(t * tk, tk)
        for h in range(hb):
            ls = slice(h * AUG, (h + 1) * AUG)
            s = s_ref[h]
            if masked:
                s = jnp.where(col + k0 <= row + qi * tq, s, -jnp.inf)
            m_prev = m_scr[h]
            m_new = jnp.maximum(m_prev, jnp.max(s, axis=-1, keepdims=True))
            alpha = jnp.exp(m_prev - m_new)
            p = jnp.exp(s - jnp.concatenate([m_new] * (tk // LANES), axis=1))
            pb = p.astype(BF16)
            vt = v_ref[pl.ds(k0, tk), ls]
            pv = _dot(pb, vt)
            if precise:
                pv = pv + _dot((p - pb.astype(F32)).astype(BF16), vt) + _dot(pb, vl_ref[pl.ds(k0, tk), ls])
            acc_scr[h] = alpha * acc_scr[h] + pv
            m_scr[h] = m_new

    npair = nfull // 2
    scores(sa_scr, 0)

    def body(i, carry):
        scores(sb_scr, 2 * i + 1)
        update(sa_scr, 2 * i, False)
        scores(sa_scr, 2 * i + 2)
        update(sb_scr, 2 * i + 1, False)
        return carry

    lax.fori_loop(0, npair, body, 0)

    @pl.when(nfull == 2 * npair)
    def _():
        update(sa_scr, nfull, True)

    @pl.when(nfull != 2 * npair)
    def _():
        scores(sb_scr, nfull)
        update(sa_scr, nfull - 1, False)
        update(sb_scr, nfull, True)
    lane = lax.broadcasted_iota(jnp.int32, (1, AUG), 1)
    outs = []
    for h in range(0, hb, 2):
        o = []
        for hh in (h, h + 1):
            acc = acc_scr[hh]
            o.append(acc / acc[:, FOX_HD:FOX_HD + 1])
        outs.append(jnp.where(lane < FOX_HD, o[0], pltpu.roll(o[1], FOX_HD, 1)))
    o_ref[...] = jnp.concatenate(outs, axis=1).astype(o_ref.dtype)


def _fox_prompt(qkv, bg, tg, n_run=None):
    tq = min(tg, 256)
    tk = min(tg, 512)
    hb = 4
    nq = tg // tq
    precise = len(qkv) == 6
    n_run = nq if n_run is None else n_run
    q_spec = pl.BlockSpec((tq, hb * AUG), lambda b, p, i: (b * nq + (nq - n_run) + i, p))
    kv_spec = pl.BlockSpec((tg, hb * AUG), lambda b, p, i: (b, p), pipeline_mode=pl.Buffered(1))
    specs = [q_spec, kv_spec, kv_spec]
    return pl.pallas_call(
        functools.partial(_fox_prompt_kernel, tq=tq, tk=tk, hb=hb, precise=precise, q_off=nq - n_run),
        grid=(bg, FOX_H // hb, n_run),
        in_specs=specs * (2 if precise else 1),
        out_specs=pl.BlockSpec((tq, hb * FOX_HD), lambda b, p, i: (b * n_run + i, p)),
        out_shape=jax.ShapeDtypeStruct((bg * n_run * tq, BRANCH_W), F32 if precise else BF16),
        scratch_shapes=[pltpu.VMEM((hb, tq, AUG), F32)] * 2 + [pltpu.VMEM((hb, tq, tk), F32)] * 2,
        compiler_params=_params('parallel', 'parallel', 'arbitrary'), name='fox_prompt')(*qkv)


def _fox_sample_kernel(pt_ref, q_ref, kn_ref, vn_ref, lfn_ref, *refs, P, tq, single):
    del pt_ref
    k_refs, v_refs, lf_refs = refs[:P], refs[P:2 * P], refs[2 * P:3 * P]
    o_ref, qbd, m_scr, l_scr, acc_scr, carry = refs[3 * P:]
    j = pl.program_id(1)
    nrow = tq * FOX_H
    hrow = lax.broadcasted_iota(jnp.int32, (FOX_H, BRANCH_W), 0)
    hcol = lax.broadcasted_iota(jnp.int32, (FOX_H, BRANCH_W), 1)
    hmask = (hcol // FOX_HD) == hrow

    @pl.when(j == 0)
    def _():
        q = q_ref[...]
        qbd[...] = jnp.concatenate(
            [jnp.where(hmask, jnp.broadcast_to(q[t:t + 1, :], (FOX_H, BRANCH_W)), 0.0) for t in range(tq)],
            axis=0).astype(BF16)
        m_scr[...] = jnp.full_like(m_scr, -jnp.inf)
        l_scr[...] = jnp.zeros_like(l_scr)
        acc_scr[...] = jnp.zeros_like(acc_scr)
        carry[...] = jnp.zeros_like(carry)

    def update(qks, lfs, own_last, pvs):
        ss = []
        cur = carry[...]
        for i, (qk, (pre, tot)) in enumerate(zip(qks, [_lane_prefix(lf) for lf in lfs])):
            sb = qk - jnp.concatenate([cur + pre] * tq, axis=0)
            if own_last and i == len(qks) - 1:
                key = lax.broadcasted_iota(jnp.int32, (nrow, PAGE_SIZE), 1)
                tok = lax.broadcasted_iota(jnp.int32, (nrow, PAGE_SIZE), 0) // FOX_H
                sb = jnp.where(key <= tok, sb, -jnp.inf)
            ss.append(sb)
            cur = cur + tot
        carry[...] = cur
        s = jnp.concatenate(ss, axis=1)
        m_prev = m_scr[...]
        m_new = jnp.maximum(m_prev, jnp.max(s, axis=-1, keepdims=True))
        alpha = jnp.exp(m_prev - m_new)
        p = jnp.exp(s - jnp.concatenate([m_new] * len(qks), axis=1))
        l_scr[...] = alpha * l_scr[...] + jnp.sum(p, axis=-1, keepdims=True)
        pb = p.astype(BF16)
        pv = None
        for i, f in enumerate(pvs):
            term = f(pb[:, i * PAGE_SIZE:(i + 1) * PAGE_SIZE])
            pv = term if pv is None else pv + term
        acc_scr[...] = jnp.concatenate([alpha] * (BRANCH_W // LANES), axis=1) * acc_scr[...] + pv
        m_scr[...] = m_new

    q_bd = qbd[...]
    kts = [k_refs[i][...].reshape(BRANCH_W, PAGE_SIZE).astype(BF16) for i in range(P)]
    vts = [v_refs[i][...].reshape(BRANCH_W, PAGE_SIZE).astype(BF16) for i in range(P)]
    page_qk = [_dot(q_bd, kt) for kt in kts]
    page_lf = [lf_refs[i][...] for i in range(P)]
    page_pv = [lambda p, vt=vt: _dot_nt(p, vt) for vt in vts]

    def own_block():
        zpad = jnp.zeros((PAGE_SIZE - tq, BRANCH_W), F32)
        kn = jnp.concatenate([kn_ref[...], zpad], axis=0).astype(BF16)
        vn = jnp.concatenate([vn_ref[...], zpad], axis=0).astype(BF16)
        return _dot_nt(q_bd, kn), lfn_ref[0], lambda p: _dot(p, vn)

    def finish():
        o = acc_scr[...] / jnp.concatenate([l_scr[...]] * (BRANCH_W // LANES), axis=1)
        rows = [jnp.sum(jnp.where(hmask, o[t * FOX_H:(t + 1) * FOX_H], 0.0), axis=0, keepdims=True)
                for t in range(tq)]
        o_ref[...] = jnp.concatenate(rows, axis=0)

    if single:
        qk_n, lf_n, pv_n = own_block()
        update(page_qk + [qk_n], page_lf + [lf_n], True, page_pv + [pv_n])
        finish()
    else:
        update(page_qk, page_lf, False, page_pv)

        @pl.when(j == pl.num_programs(1) - 1)
        def _():
            qk_n, lf_n, pv_n = own_block()
            update([qk_n], [lf_n], True, [pv_n])
            finish()


def _fox_sample(fq, fk32, fv32, lft, layer, k_cache, v_cache, lf_cache, page_table, db, tq):
    n_pages = page_table.shape[1]
    P = next(p for p in (16, 8, 4, 2, 1) if n_pages % p == 0)
    assert tq == SUBLANES
    kt = jnp.transpose(k_cache, (0, 1, 3, 4, 2))
    vt = jnp.transpose(v_cache, (0, 1, 3, 4, 2))
    lfp = jnp.swapaxes(lf_cache, 2, 3)
    lfn = jnp.swapaxes(lft.reshape(FOX_H, db, tq), 0, 1)
    lfn = jnp.pad(lfn, ((0, 0), (0, 0), (0, PAGE_SIZE - tq)))
    tok = lambda: pl.BlockSpec((tq, BRANCH_W), lambda b, j, pt: (b, 0))
    page = lambda i: pl.BlockSpec((None, None, FOX_H, FOX_HD, PAGE_SIZE),
                                  lambda b, j, pt, i=i: (layer, pt[b, j * P + i], 0, 0, 0))
    lfpage = lambda i: pl.BlockSpec((None, None, FOX_H, PAGE_SIZE),
                                    lambda b, j, pt, i=i: (layer, pt[b, j * P + i], 0, 0))
    nrow = tq * FOX_H
    return pl.pallas_call(
        functools.partial(_fox_sample_kernel, P=P, tq=tq, single=n_pages == P),
        grid_spec=pltpu.PrefetchScalarGridSpec(
            num_scalar_prefetch=1, grid=(db, n_pages // P),
            in_specs=[tok(), tok(), tok(), pl.BlockSpec((1, FOX_H, PAGE_SIZE), lambda b, j, pt: (b, 0, 0))]
                     + [page(i) for i in range(P)] + [page(i) for i in range(P)]
                     + [lfpage(i) for i in range(P)],
            out_specs=tok(),
            scratch_shapes=[pltpu.VMEM((nrow, BRANCH_W), BF16), pltpu.VMEM((nrow, LANES), F32),
                            pltpu.VMEM((nrow, LANES), F32), pltpu.VMEM((nrow, BRANCH_W), F32),
                            pltpu.VMEM((FOX_H, LANES), F32)]),
        out_shape=jax.ShapeDtypeStruct((db * tq, BRANCH_W), F32),
        compiler_params=_params('parallel', 'arbitrary'), name='fox_sample')(
            page_table, fq, fk32, fv32, lfn, *([kt] * P), *([vt] * P), *([lfp] * P))


def _merge_kernel(x_ref, og_ref, of_ref, oc_ref, wg_ref, wb_ref, wo_ref, g_ref, b_ref, y_ref, *, precise):
    x = x_ref[...]
    xs = _pieces(x, precise)
    m = None
    for jb, o_ref in enumerate((og_ref, of_ref, oc_ref)):
        gj = _sigmoid(_mm(xs, wg_ref.at[:, :, jb * D_MODEL:(jb + 1) * D_MODEL]))
        term = gj * _mm(_pieces(o_ref[...].astype(F32), precise), wb_ref.at[:, jb])
        m = term if m is None else m + term
    y = DEEPNORM_ALPHA * x + _mm(_pieces(m, precise), wo_ref)
    y_ref[...] = _layernorm(y, g_ref[...], b_ref[...])


def _merge(x, o_gla, o_fox, o_conv, w, precise):
    rows = x.shape[0]
    tm = _row_tile(rows, 512)
    row = lambda n: pl.BlockSpec((tm, n), lambda i: (i, 0))
    consts = [w['wg'], w['wb'], w['wo'], w['ln1_g'], w['ln1_b']]
    return pl.pallas_call(
        functools.partial(_merge_kernel, precise=precise), grid=(rows // tm,),
        in_specs=[row(D_MODEL), row(BRANCH_W), row(BRANCH_W), row(BRANCH_W)]
                 + [_const_spec(a.shape) for a in consts],
        out_specs=row(D_MODEL), out_shape=jax.ShapeDtypeStruct((rows, D_MODEL), F32),
        compiler_params=_params('parallel'), name='merge')(x, o_gla, o_fox, o_conv, *consts)


def _route(probs):
    rows = [probs[i:i + 1, :] for i in range(N_EXPERTS)]
    gscore = []
    for g in range(N_GROUPS):
        r0, r1, r2, r3 = rows[EXP_PER_GROUP * g:EXP_PER_GROUP * (g + 1)]
        a, b = jnp.maximum(r0, r1), jnp.minimum(r0, r1)
        c, d = jnp.maximum(r2, r3), jnp.minimum(r2, r3)
        gscore.append(jnp.maximum(a, c) + jnp.maximum(jnp.minimum(a, c), jnp.maximum(b, d)))
    best, gsel = gscore[0], jnp.zeros_like(gscore[0], dtype=jnp.int32)
    for g in range(1, N_GROUPS):
        upd = gscore[g] > best
        best = jnp.where(upd, gscore[g], best)
        gsel = jnp.where(upd, g, gsel)
    v = []
    for i in range(EXP_PER_GROUP):
        vi = rows[(N_GROUPS - 1) * EXP_PER_GROUP + i]
        for g in range(N_GROUPS - 2, -1, -1):
            vi = jnp.where(gsel == g, rows[g * EXP_PER_GROUP + i], vi)
        v.append(vi)
    v1, i1 = v[0], jnp.zeros_like(gsel)
    for i in range(1, EXP_PER_GROUP):
        upd = v[i] > v1
        v1 = jnp.where(upd, v[i], v1)
        i1 = jnp.where(upd, i, i1)
    v2, i2 = jnp.full_like(v1, -1.0), jnp.full_like(gsel, -1)
    for i in range(EXP_PER_GROUP):
        cand = jnp.where(i1 == i, -1.0, v[i])
        upd = cand > v2
        v2 = jnp.where(upd, cand, v2)
        i2 = jnp.where(upd, i, i2)
    den = v1 + v2
    w1, w2 = v1 / den, v2 / den
    out = []
    for g in range(N_GROUPS):
        for i in range(EXP_PER_GROUP):
            wi = jnp.where(i1 == i, w1, jnp.where(i2 == i, w2, 0.0))
            out.append(jnp.where(gsel == g, wi, 0.0))
    return out


def _moe_kernel(x_ref, wrt_ref, br_ref, wg_ref, wu_ref, wd_ref, g_ref, b_ref, y_ref):
    x = x_ref[...]
    tm = x.shape[0]
    xb = x.astype(BF16)
    x_lo = (x - xb.astype(F32)).astype(BF16)
    wrt = wrt_ref[...]
    w_hi = wrt.astype(BF16)
    w_lo = (wrt - w_hi.astype(F32)).astype(BF16)
    logits = _dot_nt(w_hi, xb) + _dot_nt(w_hi, x_lo) + _dot_nt(w_lo, xb) + br_ref[...]
    e = jnp.exp(logits - jnp.max(logits, axis=0, keepdims=True))
    probs = e / jnp.sum(e, axis=0, keepdims=True)
    gate_rows = _route(probs)
    gate_t = jnp.concatenate(gate_rows + [jnp.zeros((LANES - N_EXPERTS, tm), F32)], axis=0)
    gate = gate_t.T
    acc = jnp.zeros((tm, D_MODEL), F32)
    for ex in range(N_EXPERTS):
        hg = _dot(xb, wg_ref[ex])
        hu = _dot(xb, wu_ref[ex])
        h = (hg * _sigmoid(hg)) * hu * gate[:, ex:ex + 1]
        acc = acc + _dot(h.astype(BF16), wd_ref[ex])
    y = DEEPNORM_ALPHA * x + acc
    y_ref[...] = _layernorm(y, g_ref[...], b_ref[...])


def _moe(x, w):
    rows = x.shape[0]
    tm = _row_tile(rows, 512)
    row = pl.BlockSpec((tm, D_MODEL), lambda i: (i, 0))
    consts = [w['wrt'], w['br'], w['weg'], w['weu'], w['wed'], w['ln2_g'], w['ln2_b']]
    return pl.pallas_call(
        _moe_kernel, grid=(rows // tm,),
        in_specs=[row] + [_const_spec(a.shape) for a in consts],
        out_specs=row, out_shape=jax.ShapeDtypeStruct((rows, D_MODEL), F32),
        compiler_params=_params('parallel'), name='moe')(x, *consts)


def _prep_weights(l, precise, w_in, w_gla_a2, b_gla_a, gla_norm_g, b_fox_f, conv_w, w_branch, w_out, ln1_g, ln1_b,
                  w_router, b_router, w_e_gate, w_e_up, w_e_down, ln2_g, ln2_b):
    wi = w_in[l]
    off = {}
    o = 0
    for name, n in (('gla_q', 256), ('gla_k', 256), ('gla_v', 512), ('gla_a', GLA_LR), ('gla_r', 512),
                    ('fox_q', 512), ('fox_k', 512), ('fox_v', 512), ('fox_f', FOX_H),
                    ('conv', 3 * CONV_D), ('gate', N_BRANCH * D_MODEL)):
        off[name] = (o, o + n)
        o += n
    col = lambda name: wi[:, off[name][0]:off[name][1]]
    padc = lambda a, n: jnp.pad(a, ((0, 0), (0, n - a.shape[1])))
    row2 = lambda a: a.reshape(1, -1)

    def bf(a):
        hi = a.astype(BF16)
        return jnp.stack([hi, (a - hi.astype(F32)).astype(BF16)]) if precise else hi[None]

    ff = col('fox_f')
    return {
        'wqkv': bf(jnp.concatenate([col('gla_q') * (GLA_DK ** -0.5), col('gla_k'), col('gla_v')], axis=1)),
        'wa': bf(padc(col('gla_a'), LANES)),
        'wa2': bf(jnp.pad(w_gla_a2[l], ((0, LANES - GLA_LR), (0, 0)))),
        'ba': row2(b_gla_a[l]),
        'wr': bf(col('gla_r')),
        'wfq': bf((col('fox_q') * (FOX_HD ** -0.5))),
        'wfk': bf(col('fox_k')),
        'wfv': bf(col('fox_v')),
        'wfkt': bf(col('fox_k').T),
        'wfvt': bf(col('fox_v').T),
        'wff': bf(padc(ff, LANES)),
        'bff': padc(row2(b_fox_f[l]), LANES),
        'wfft': bf(jnp.pad(ff.T, ((0, 2 * SUBLANES - FOX_H), (0, 0)))),
        'bfft': b_fox_f[l].reshape(FOX_H, 1),
        'pad_ones': (jnp.arange(BRANCH_W)[None, :] % FOX_HD < jnp.arange(1, SUBLANES + 1)[:, None]).astype(F32),
        'wc': bf(col('conv')),
        'wg': bf(col('gate')),
        'gnorm': row2(gla_norm_g[l]),
        'conv_w': conv_w[l],
        'wb': bf(w_branch[l]),
        'wo': bf(w_out[l]),
        'ln1_g': row2(ln1_g[l]), 'ln1_b': row2(ln1_b[l]),
        'wrt': w_router.T, 'br': b_router.reshape(N_EXPERTS, 1),
        'weg': w_e_gate[l].astype(BF16), 'weu': w_e_up[l].astype(BF16), 'wed': w_e_down[l].astype(BF16),
        'ln2_g': row2(ln2_g[l]), 'ln2_b': row2(ln2_b[l]),
    }


TAIL = 256
SPLICE = 128


def _layer(x, w, bg, tg, gla_s0, conv_buf, pools, act_dtype, precise):
    prompt = pools is None
    precise = precise and prompt and tg >= 2 * TAIL
    gqkv, gla, gr, fk, fv, lf, *fox_in = _proj1(x, w, prompt, tg, precise)
    bch = _proj2(x, w, False)
    o_gla, gla_state = _gla(gqkv, gla, gr, gla_s0, w['gnorm'], bg, tg, F32 if precise else act_dtype, precise)
    if prompt:
        o_fox = _fox_prompt(fox_in[:3], bg, tg)
        caches = (fk.reshape(bg, FOX_H, FOX_HD, tg), fv.reshape(bg, FOX_H, FOX_HD, tg), lf)
    else:
        o_fox = _fox_sample(fox_in[0], fk, fv, fox_in[1], *pools, bg, tg)
        caches = (fk.reshape(bg, tg, FOX_H, FOX_HD), fv.reshape(bg, tg, FOX_H, FOX_HD), lf.reshape(bg, tg, FOX_H))
    o_conv, conv_state = _conv(bch, conv_buf, w['conv_w'], bg, tg, act_dtype)
    y = _moe(_merge(x, o_gla, o_fox, o_conv, w, False), w)
    if precise:
        n_t = TAIL // min(tg, 256)
        tail = lambda a: a.reshape(bg, tg, a.shape[-1])[:, tg - TAIL:].reshape(bg * TAIL, a.shape[-1])
        xt = tail(x)
        bch_t = _proj2(xt, w, True)
        o_fox_t = _fox_prompt(fox_in, bg, tg, n_run=n_t)
        o_conv_t, _ = _conv(bch_t, jnp.zeros_like(conv_buf), w['conv_w'], bg, TAIL, F32)
        y_t = _moe(_merge(xt, tail(o_gla), o_fox_t, o_conv_t, w, True), w)
        y = y.reshape(bg, tg, D_MODEL).at[:, tg - SPLICE:].set(
            y_t.reshape(bg, TAIL, D_MODEL)[:, TAIL - SPLICE:]).reshape(bg * tg, D_MODEL)
    return y, caches + (gla_state, conv_state)


def kernel(x_prompt, x_sample, cache_fox_k, cache_fox_v, cache_fox_logf, state_gla, state_conv, page_table, w_in, w_gla_a2, b_gla_a, gla_norm_g, b_fox_f, conv_w, w_branch, w_out, ln1_g, ln1_b, w_router, b_router, w_e_gate, w_e_up, w_e_down, ln2_g, ln2_b):
    B, T, _ = x_prompt.shape
    DB, TS, _ = x_sample.shape
    xp = x_prompt.reshape(B * T, D_MODEL)
    xs = x_sample.reshape(DB * TS, D_MODEL)
    depth = w_in.shape[0]
    st_p, st_s = [], []
    for l in range(depth):
        precise = l < depth - 1
        w = _prep_weights(l, precise, w_in, w_gla_a2, b_gla_a, gla_norm_g, b_fox_f, conv_w, w_branch, w_out,
                          ln1_g, ln1_b, w_router, b_router, w_e_gate, w_e_up, w_e_down, ln2_g, ln2_b)
        xp, sp = _layer(xp, w, B, T, jnp.zeros((B, GLA_H, GLA_DK, GLA_DV), F32),
                        jnp.zeros((B, CONV_W - 1, CONV_D), F32), None, BF16, precise)
        xs, ss = _layer(xs, w, DB, TS, state_gla[l], state_conv[l],
                        (l, cache_fox_k, cache_fox_v, cache_fox_logf, page_table), F32, False)
        st_p.append(sp)
        st_s.append(ss)
    stack = lambda sts, i: jnp.stack([s[i] for s in sts])
    return (xp.reshape(B, T, D_MODEL), xs.reshape(DB, TS, D_MODEL),
            jnp.transpose(stack(st_p, 0), (0, 1, 4, 2, 3)), jnp.transpose(stack(st_p, 1), (0, 1, 4, 2, 3)),
            jnp.swapaxes(stack(st_p, 2), 2, 3), stack(st_p, 3), stack(st_p, 4),
            stack(st_s, 0), stack(st_s, 1), stack(st_s, 2), stack(st_s, 3), stack(st_s, 4))
```

```python
import functools

import jax
import jax.numpy as jnp
from jax import lax
from jax.experimental import pallas as pl
from jax.experimental.pallas import tpu as pltpu

F32 = jnp.float32
BF16 = jnp.bfloat16

D_MODEL = 1024
DEPTH = 2
PAGE_SIZE = 128
BRANCH_W = D_MODEL // 2
GLA_H = 4
GLA_DK = BRANCH_W // (2 * GLA_H)
GLA_DV = BRANCH_W // GLA_H
GLA_LR = 16
GLA_TAU = 16.0
GLA_CHUNK = 64
FOX_HD = 64
FOX_H = BRANCH_W // FOX_HD
CONV_D = BRANCH_W
CONV_W = 3
N_BRANCH = 3
N_EXPERTS = 16
N_GROUPS = 4
EXP_PER_GROUP = N_EXPERTS // N_GROUPS
D_EXPERT = D_MODEL // 4
LN_EPS = 1e-5
DEEPNORM_ALPHA = (2 * DEPTH) ** 0.25

LANES = 128
SUBLANES = 8
VMEM_LIMIT = 56 * 1024 * 1024

_NT = (((1,), (1,)), ((), ()))
_TN = (((0,), (0,)), ((), ()))


def _dot(a, b):
    return jnp.dot(a, b, preferred_element_type=F32)


def _dot_nt(a, b):
    return lax.dot_general(a, b, _NT, preferred_element_type=F32)


def _dot_tn(a, b):
    return lax.dot_general(a, b, _TN, preferred_element_type=F32)


def _split3(x):
    hi = x.astype(BF16)
    r1 = x - hi.astype(F32)
    mid = r1.astype(BF16)
    lo = (r1 - mid.astype(F32)).astype(BF16)
    return hi, mid, lo


def _pieces(x, precise):
    hi = x.astype(BF16)
    if not precise:
        return (hi,)
    return hi, (x - hi.astype(F32)).astype(BF16)


def _mm(xs, w_ref):
    y = _dot(xs[0], w_ref[0])
    if len(xs) == 2:
        y = y + _dot(xs[1], w_ref[0]) + _dot(xs[0], w_ref[1])
    return y


def _mm_nt(w_ref, xs):
    y = _dot_nt(w_ref[0], xs[0])
    if len(xs) == 2:
        y = y + _dot_nt(w_ref[0], xs[1]) + _dot_nt(w_ref[1], xs[0])
    return y


def _pdot(dot, a, b, precise):
    ap, bp = _pieces(a, precise), _pieces(b, precise)
    y = dot(ap[0], bp[0])
    if precise:
        y = y + dot(ap[1], bp[0]) + dot(ap[0], bp[1])
    return y


def _log_sigmoid(x):
    return jnp.minimum(x, 0.0) - jnp.log1p(jnp.exp(-jnp.abs(x)))


def _sigmoid(x):
    return 1.0 / (1.0 + jnp.exp(-x))


def _layernorm(y, g, b):
    mu = jnp.mean(y, axis=-1, keepdims=True)
    d = y - mu
    var = jnp.mean(d * d, axis=-1, keepdims=True)
    return d * lax.rsqrt(var + LN_EPS) * g + b


def _lane_prefix(x):
    r = lax.broadcasted_iota(jnp.int32, (LANES, 2 * LANES), 0)
    c = lax.broadcasted_iota(jnp.int32, (LANES, 2 * LANES), 1)
    u = jnp.where(r <= c, 1.0, 0.0).astype(BF16)
    hi, mid, lo = _split3(x)
    y = _dot(hi, u) + _dot(mid, u) + _dot(lo, u)
    return y[:, :LANES], y[:, LANES:]


def _const_spec(shape):
    nd = len(shape)
    return pl.BlockSpec(shape, lambda *_: (0,) * nd, pipeline_mode=pl.Buffered(1))


def _params(*sem):
    return pltpu.CompilerParams(dimension_semantics=sem, vmem_limit_bytes=VMEM_LIMIT)


def _row_tile(rows, want):
    t = min(rows, want)
    assert rows % t == 0
    return t


AUG = 2 * FOX_HD
AUG_W = FOX_H * AUG


def _mm_rows(tail, xs, w_ref, store):
    if tail is None or len(xs) == 1:
        store(_mm(xs, w_ref))
        return

    @pl.when(tail)
    def _():
        store(_mm(xs, w_ref))

    @pl.when(jnp.logical_not(tail))
    def _():
        store(_mm(xs[:1], w_ref))


def _gla_cols(xs, tail, wqkv_ref, wa_ref, wa2_ref, ba_ref, wr_ref, gqkv_ref, gla_ref, gr_ref):
    nq = GLA_H * GLA_DK

    def put_q(y):
        gqkv_ref[:, :nq] = y

    def put_r(r):
        gr_ref[...] = r * _sigmoid(r)

    _mm_rows(tail, xs, wqkv_ref.at[:, :, :nq], put_q)
    gqkv_ref[:, nq:] = _mm(xs, wqkv_ref.at[:, :, nq:])
    a_pre = _mm(_pieces(_mm(xs, wa_ref), len(xs) == 2), wa2_ref) + ba_ref[...]
    gla_ref[...] = _log_sigmoid(a_pre) * (1.0 / GLA_TAU)
    _mm_rows(tail, xs, wr_ref, put_r)


def _proj1_prompt_kernel(x_ref, wqkv_ref, wa_ref, wa2_ref, ba_ref, wr_ref, wfk_ref, wfv_ref, wff_ref, bff_ref,
                         wfft_ref, bfft_ref, wqa_ref, wka_ref, wva_ref, pad_ref,
                         gqkv_ref, gla_ref, gr_ref, fk_ref, fv_ref, lf_ref, qa_ref, ka_ref, va_ref,
                         *rest, tiles_per_seq, tail_tiles, precise):
    tm = x_ref.shape[0]
    lo_refs, carry = rest[:-1], rest[-1]

    @pl.when(pl.program_id(0) % tiles_per_seq == 0)
    def _():
        carry[...] = jnp.zeros_like(carry)

    xs = _pieces(x_ref[...], precise)
    tail = (pl.program_id(0) % tiles_per_seq >= tiles_per_seq - tail_tiles) if precise else None
    _gla_cols(xs, tail, wqkv_ref, wa_ref, wa2_ref, ba_ref, wr_ref, gqkv_ref, gla_ref, gr_ref)
    fk_ref[...] = _mm_nt(wfk_ref, xs[:1])
    fv_ref[...] = _mm_nt(wfv_ref, xs[:1])
    lf_ref[...] = _log_sigmoid(_mm_nt(wfft_ref, xs[:1])[:FOX_H] + bfft_ref[...])
    lf = _log_sigmoid(_mm(xs, wff_ref) + bff_ref[...])
    ri = lax.broadcasted_iota(jnp.int32, (tm, tm), 0)
    ci = lax.broadcasted_iota(jnp.int32, (tm, tm), 1)
    lmat = jnp.where(ci <= ri, 1.0, 0.0).astype(BF16)
    hi, mid, lo = _split3(lf)
    c = _dot(lmat, hi) + _dot(lmat, mid) + _dot(lmat, lo) + carry[0:1, :]
    carry[...] = jnp.broadcast_to(c[tm - 1:tm, :], carry.shape)
    pr = lax.broadcasted_iota(jnp.int32, (LANES, BRANCH_W), 0)
    pc = lax.broadcasted_iota(jnp.int32, (LANES, BRANCH_W), 1)
    kpad = None
    for j, part in enumerate(_split3(-c)):
        term = _dot(part, jnp.where(pc == pr * FOX_HD + j, 1.0, 0.0).astype(BF16))
        kpad = term if kpad is None else kpad + term
    ones = lambda n: jnp.broadcast_to(pad_ref[n - 1:n, :], (tm, BRANCH_W))

    def put(i, val, pad, hi_ref):
        parts = []
        for h in range(FOX_H):
            sl = slice(h * FOX_HD, (h + 1) * FOX_HD)
            parts += [val[:, sl], pad[:, sl]]
        aug = jnp.concatenate(parts, axis=1)
        hi = aug.astype(BF16)
        hi_ref[...] = hi
        if precise:
            lo_refs[i][...] = (aug - hi.astype(F32)).astype(BF16)

    put(1, _mm(xs, wka_ref), kpad, ka_ref)
    put(2, _mm(xs, wva_ref), ones(1), va_ref)
    _mm_rows(tail, xs, wqa_ref, lambda q: put(0, q, ones(3), qa_ref))


def _proj1_sample_kernel(x_ref, wqkv_ref, wa_ref, wa2_ref, ba_ref, wr_ref, wfk_ref, wfv_ref, wff_ref, bff_ref,
                         wfft_ref, bfft_ref, wfq_ref,
                         gqkv_ref, gla_ref, gr_ref, fk_ref, fv_ref, lf_ref, fq_ref, lft_ref):
    xs = _pieces(x_ref[...], False)
    _gla_cols(xs, None, wqkv_ref, wa_ref, wa2_ref, ba_ref, wr_ref, gqkv_ref, gla_ref, gr_ref)
    fk_ref[...] = _mm(xs, wfk_ref)
    fv_ref[...] = _mm(xs, wfv_ref)
    lf_ref[...] = _log_sigmoid(_mm(xs, wff_ref) + bff_ref[...])[:, :FOX_H]
    fq_ref[...] = _mm(xs, wfq_ref)
    lft_ref[...] = _log_sigmoid(_mm_nt(wfft_ref, xs)[:FOX_H] + bfft_ref[...])


def _proj1(x, w, prompt, tg, precise):
    rows = x.shape[0]
    tm = _row_tile(rows, 256)
    row = lambda n: pl.BlockSpec((tm, n), lambda i: (i, 0))
    common = ('wqkv', 'wa', 'wa2', 'ba', 'wr', 'wfkt' if prompt else 'wfk', 'wfvt' if prompt else 'wfv',
              'wff', 'bff', 'wfft', 'bfft')
    sds = lambda n, dt: jax.ShapeDtypeStruct((rows, n), dt)
    out_shape = [sds(1024, F32), sds(256, F32), sds(512, F32)]
    out_specs = [row(1024), row(256), row(512)]
    if prompt:
        assert tg % tm == 0
        tps = tg // tm
        names = common + ('wfq', 'wfk', 'wfv', 'pad_ones')
        kern = functools.partial(_proj1_prompt_kernel, tiles_per_seq=tps, tail_tiles=-(-TAIL // tm),
                                 precise=precise)
        tr = lambda n: pl.BlockSpec((None, n, tm), lambda i: (i // tps, 0, i % tps))
        trs = lambda n: jax.ShapeDtypeStruct((rows // tg, n, tg), F32)
        n_aug = 6 if precise else 3
        out_shape += [trs(512), trs(512), trs(FOX_H)] + [sds(AUG_W, BF16)] * n_aug
        out_specs += [tr(512), tr(512), tr(FOX_H)] + [row(AUG_W)] * n_aug
        scratch = [pltpu.VMEM((SUBLANES, LANES), F32)]
    else:
        names = common + ('wfq',)
        kern = _proj1_sample_kernel
        out_shape += [sds(512, F32), sds(512, F32), sds(FOX_H, F32), sds(512, F32),
                      jax.ShapeDtypeStruct((FOX_H, rows), F32)]
        out_specs += [row(512), row(512), row(FOX_H), row(512), pl.BlockSpec((FOX_H, tm), lambda i: (0, i))]
        scratch = []
    ws = [w[n] for n in names]
    return pl.pallas_call(
        kern, grid=(rows // tm,),
        in_specs=[row(D_MODEL)] + [_const_spec(a.shape) for a in ws],
        out_specs=out_specs, out_shape=out_shape, scratch_shapes=scratch,
        compiler_params=_params('arbitrary'), name='proj1_prompt' if prompt else 'proj1_sample')(x, *ws)


def _gla_kernel(qkv_ref, la_ref, r_ref, s0_ref, g_ref, o_ref, sfin_ref, state, *, G, cr, cp, nchunk, precise):
    ti = pl.program_id(1)

    @pl.when(ti == 0)
    def _():
        state[...] = s0_ref[...]

    ri = lax.broadcasted_iota(jnp.int32, (cp, cp), 0)
    ci = lax.broadcasted_iota(jnp.int32, (cp, cp), 1)
    tril = ci <= ri
    lmat = jnp.where(tril, 1.0, 0.0).astype(BF16)
    ones = jnp.ones((cp, GLA_DV), BF16)
    gnorm = g_ref[...]

    def pad(a):
        if cp == cr:
            return a
        return jnp.concatenate([a, jnp.zeros((cp - cr, a.shape[1]), a.dtype)], axis=0)

    def chunk(c, carry):
        r0 = pl.multiple_of(c * cr, cr)
        for g in range(G):
            qkv = pad(qkv_ref[g, pl.ds(r0, cr), :])
            la = pad(la_ref[g, pl.ds(r0, cr), :])
            q = qkv[:, :GLA_H * GLA_DK]
            k = qkv[:, GLA_H * GLA_DK:2 * GLA_H * GLA_DK]
            v = qkv[:, 2 * GLA_H * GLA_DK:]
            la3 = _split3(la)
            b = _dot(lmat, la3[0]) + _dot(lmat, la3[1]) + _dot(lmat, la3[2])
            b_last = b[cp - 1:cp, :]
            q_in = q * jnp.exp(b)
            k_in = k * jnp.exp(-b)
            k_end = k * jnp.exp(b_last - b)
            outs = []
            for h in range(GLA_H):
                ks = slice(h * GLA_DK, (h + 1) * GLA_DK)
                vs = slice(h * GLA_DV, (h + 1) * GLA_DV)
                s_prev = state[g, h]
                att = jnp.where(tril, _pdot(_dot_nt, q_in[:, ks], k_in[:, ks], precise), 0.0)
                o = _pdot(_dot, att, v[:, vs], precise) + _pdot(_dot, q_in[:, ks], s_prev, precise)
                dlog = (_dot_tn(la3[0][:, ks], ones) + _dot_tn(la3[1][:, ks], ones)
                        + _dot_tn(la3[2][:, ks], ones))
                state[g, h] = jnp.exp(dlog) * s_prev + _pdot(_dot_tn, k_end[:, ks], v[:, vs], precise)
                ms = jnp.mean(o * o, axis=-1, keepdims=True)
                outs.append(o * lax.rsqrt(ms + LN_EPS) * gnorm[:, vs])
            o_all = jnp.concatenate(outs, axis=1)[:cr] * r_ref[g, pl.ds(r0, cr), :]
            o_ref[g, pl.ds(r0, cr), :] = o_all.astype(o_ref.dtype)
        return carry

    lax.fori_loop(0, nchunk, chunk, 0, unroll=4 if nchunk % 4 == 0 else 1)

    @pl.when(ti == pl.num_programs(1) - 1)
    def _():
        sfin_ref[...] = state[...]


def _gla(gqkv, gla, gr, s0, gnorm, bg, tg, out_dtype, precise):
    if tg % GLA_CHUNK == 0:
        G, cr, cp, tt = 2 if bg % 2 == 0 else 1, GLA_CHUNK, GLA_CHUNK, min(tg, 512)
    else:
        assert tg % SUBLANES == 0 and tg <= 2 * SUBLANES
        G, cr, cp, tt = SUBLANES, tg, 2 * SUBLANES, tg
    assert bg % G == 0 and tg % tt == 0
    blk = lambda n: pl.BlockSpec((G, tt, n), lambda b, t: (b, t, 0))
    sblk = pl.BlockSpec((G, GLA_H, GLA_DK, GLA_DV), lambda b, t: (b, 0, 0, 0))
    kern = functools.partial(_gla_kernel, G=G, cr=cr, cp=cp, nchunk=tt // cr, precise=precise)
    o, sfin = pl.pallas_call(
        kern, grid=(bg // G, tg // tt),
        in_specs=[blk(1024), blk(256), blk(512), sblk, _const_spec(gnorm.shape)],
        out_specs=(blk(512), sblk),
        out_shape=(jax.ShapeDtypeStruct((bg, tg, 512), out_dtype),
                   jax.ShapeDtypeStruct((bg, GLA_H, GLA_DK, GLA_DV), F32)),
        scratch_shapes=[pltpu.VMEM((G, GLA_H, GLA_DK, GLA_DV), F32)],
        compiler_params=_params('parallel', 'arbitrary'), name='gla')(
            gqkv.reshape(bg, tg, 1024), gla.reshape(bg, tg, 256), gr.reshape(bg, tg, 512), s0, gnorm)
    return o.reshape(bg * tg, 512), sfin


def _conv_kernel(x_ref, wc_ref, buf_ref, w_ref, o_ref, st_ref, prev, *, G, tt, precise):
    @pl.when(pl.program_id(1) == 0)
    def _():
        prev[...] = buf_ref[...]

    w = w_ref[...]
    rowi = lax.broadcasted_iota(jnp.int32, (tt, CONV_D), 0)
    bch_all = _mm(_pieces(x_ref[...].reshape(G * tt, D_MODEL), precise), wc_ref)
    for g in range(G):
        bch = bch_all[g * tt:(g + 1) * tt]
        u = bch[:, CONV_D:2 * CONV_D] * bch[:, 2 * CONV_D:]
        p = prev[g]
        u1 = jnp.where(rowi == 0, p[1:2], pltpu.roll(u, 1, 0))
        u2 = jnp.where(rowi == 0, p[0:1], jnp.where(rowi == 1, p[1:2], pltpu.roll(u, 2, 0)))
        y = w[0:1] * u2 + w[1:2] * u1 + w[2:3] * u
        o_ref[g] = (bch[:, :CONV_D] * y).astype(o_ref.dtype)
        tail = u[tt - (CONV_W - 1):tt]
        prev[g] = tail
        st_ref[g] = tail


def _conv(x, wc, buf, conv_w, bg, tg, out_dtype, precise):
    if tg >= 512:
        G, tt = 1, 512
    else:
        G, tt = min(bg, 16), tg
    assert bg % G == 0 and tg % tt == 0 and tt >= CONV_W - 1
    sblk = pl.BlockSpec((G, CONV_W - 1, CONV_D), lambda b, t: (b, 0, 0))
    o, st = pl.pallas_call(
        functools.partial(_conv_kernel, G=G, tt=tt, precise=precise), grid=(bg // G, tg // tt),
        in_specs=[pl.BlockSpec((G, tt, D_MODEL), lambda b, t: (b, t, 0)), _const_spec(wc.shape), sblk,
                  _const_spec(conv_w.shape)],
        out_specs=(pl.BlockSpec((G, tt, CONV_D), lambda b, t: (b, t, 0)), sblk),
        out_shape=(jax.ShapeDtypeStruct((bg, tg, CONV_D), out_dtype),
                   jax.ShapeDtypeStruct((bg, CONV_W - 1, CONV_D), F32)),
        scratch_shapes=[pltpu.VMEM((G, CONV_W - 1, CONV_D), F32)],
        compiler_params=_params('parallel', 'arbitrary'), name='conv')(
            x.reshape(bg, tg, D_MODEL), wc, buf, conv_w)
    return o.reshape(bg * tg, CONV_D), st


def _fox_prompt_kernel(*refs, tq, tk, hb, precise, q_off):
    if precise:
        q_ref, k_ref, v_ref, ql_ref, kl_ref, vl_ref, o_ref, m_scr, acc_scr, sa_scr, sb_scr = refs
    else:
        q_ref, k_ref, v_ref, o_ref, m_scr, acc_scr, sa_scr, sb_scr = refs
    qi = pl.program_id(2) + q_off
    nfull = (qi * tq) // tk
    row = lax.broadcasted_iota(jnp.int32, (tq, tk), 0)
    col = lax.broadcasted_iota(jnp.int32, (tq, tk), 1)
    m_scr[...] = jnp.full_like(m_scr, -jnp.inf)
    acc_scr[...] = jnp.zeros_like(acc_scr)

    def scores(s_ref, t):
        k0 = pl.multiple_of(t * tk, tk)
        for h in range(hb):
            ls = slice(h * AUG, (h + 1) * AUG)
            kt = k_ref[pl.ds(k0, tk), ls]
            s = _dot_nt(q_ref[:, ls], kt)
            if precise:
                s = s + _dot_nt(ql_ref[:, ls], kt) + _dot_nt(q_ref[:, ls], kl_ref[pl.ds(k0, tk), ls])
            s_ref[h] = s

    def update(s_ref, t, masked):
        k0 = pl.multiple_of(t * tk, tk)
        for h in range(hb):
            ls = slice(h * AUG, (h + 1) * AUG)
            s = s_ref[h]
            if masked:
                s = jnp.where(col + k0 <= row + qi * tq, s, -jnp.inf)
            m_prev = m_scr[h]
            m_new = jnp.maximum(m_prev, jnp.max(s, axis=-1, keepdims=True))
            alpha = jnp.exp(m_prev - m_new)
            p = jnp.exp(s - jnp.concatenate([m_new] * (tk // LANES), axis=1))
            pb = p.astype(BF16)
            vt = v_ref[pl.ds(k0, tk), ls]
            pv = _dot(pb, vt)
            if precise:
                pv = pv + _dot((p - pb.astype(F32)).astype(BF16), vt) + _dot(pb, vl_ref[pl.ds(k0, tk), ls])
            acc_scr[h] = alpha * acc_scr[h] + pv
            m_scr[h] = m_new

    npair = nfull // 2
    scores(sa_scr, 0)

    def body(i, carry):
        scores(sb_scr, 2 * i + 1)
        update(sa_scr, 2 * i, False)
        scores(sa_scr, 2 * i + 2)
        update(sb_scr, 2 * i + 1, False)
        return carry

    lax.fori_loop(0, npair, body, 0)

    @pl.when(nfull == 2 * npair)
    def _():
        update(sa_scr, nfull, True)

    @pl.when(nfull != 2 * npair)
    def _():
        scores(sb_scr, nfull)
        update(sa_scr, nfull - 1, False)
        update(sb_scr, nfull, True)
    lane = lax.broadcasted_iota(jnp.int32, (1, AUG), 1)
    outs = []
    for h in range(0, hb, 2):
        o = []
        for hh in (h, h + 1):
            acc = acc_scr[hh]
            o.append(acc / acc[:, FOX_HD:FOX_HD + 1])
        outs.append(jnp.where(lane < FOX_HD, o[0], pltpu.roll(o[1], FOX_HD, 1)))
    o_ref[...] = jnp.concatenate(outs, axis=1).astype(o_ref.dtype)


def _fox_prompt(qkv, bg, tg, n_run=None):
    tq = min(tg, 256)
    tk = min(tg, 512)
    hb = 4
    nq = tg // tq
    precise = len(qkv) == 6
    n_run = nq if n_run is None else n_run
    q_spec = pl.BlockSpec((tq, hb * AUG), lambda b, p, i: (b * nq + (nq - n_run) + i, p))
    kv_spec = pl.BlockSpec((tg, hb * AUG), lambda b, p, i: (b, p), pipeline_mode=pl.Buffered(1))
    specs = [q_spec, kv_spec, kv_spec]
    return pl.pallas_call(
        functools.partial(_fox_prompt_kernel, tq=tq, tk=tk, hb=hb, precise=precise, q_off=nq - n_run),
        grid=(bg, FOX_H // hb, n_run),
        in_specs=specs * (2 if precise else 1),
        out_specs=pl.BlockSpec((tq, hb * FOX_HD), lambda b, p, i: (b * n_run + i, p)),
        out_shape=jax.ShapeDtypeStruct((bg * n_run * tq, BRANCH_W), F32 if precise else BF16),
        scratch_shapes=[pltpu.VMEM((hb, tq, AUG), F32)] * 2 + [pltpu.VMEM((hb, tq, tk), F32)] * 2,
        compiler_params=_params('parallel', 'parallel', 'arbitrary'), name='fox_prompt')(*qkv)


def _fox_sample_kernel(pt_ref, q_ref, kn_ref, vn_ref, lfn_ref, *refs, P, tq, single):
    del pt_ref
    k_refs, v_refs, lf_refs = refs[:P], refs[P:2 * P], refs[2 * P:3 * P]
    o_ref, qbd, m_scr, l_scr, acc_scr, carry = refs[3 * P:]
    j = pl.program_id(1)
    nrow = tq * FOX_H
    hrow = lax.broadcasted_iota(jnp.int32, (FOX_H, BRANCH_W), 0)
    hcol = lax.broadcasted_iota(jnp.int32, (FOX_H, BRANCH_W), 1)
    hmask = (hcol // FOX_HD) == hrow

    @pl.when(j == 0)
    def _():
        q = q_ref[...]
        qbd[...] = jnp.concatenate(
            [jnp.where(hmask, jnp.broadcast_to(q[t:t + 1, :], (FOX_H, BRANCH_W)), 0.0) for t in range(tq)],
            axis=0).astype(BF16)
        m_scr[...] = jnp.full_like(m_scr, -jnp.inf)
        l_scr[...] = jnp.zeros_like(l_scr)
        acc_scr[...] = jnp.zeros_like(acc_scr)
        carry[...] = jnp.zeros_like(carry)

    def update(qks, lfs, own_last, pvs):
        ss = []
        cur = carry[...]
        for i, (qk, (pre, tot)) in enumerate(zip(qks, [_lane_prefix(lf) for lf in lfs])):
            sb = qk - jnp.concatenate([cur + pre] * tq, axis=0)
            if own_last and i == len(qks) - 1:
                key = lax.broadcasted_iota(jnp.int32, (nrow, PAGE_SIZE), 1)
                tok = lax.broadcasted_iota(jnp.int32, (nrow, PAGE_SIZE), 0) // FOX_H
                sb = jnp.where(key <= tok, sb, -jnp.inf)
            ss.append(sb)
            cur = cur + tot
        carry[...] = cur
        s = jnp.concatenate(ss, axis=1)
        m_prev = m_scr[...]
        m_new = jnp.maximum(m_prev, jnp.max(s, axis=-1, keepdims=True))
        alpha = jnp.exp(m_prev - m_new)
        p = jnp.exp(s - jnp.concatenate([m_new] * len(qks), axis=1))
        l_scr[...] = alpha * l_scr[...] + jnp.sum(p, axis=-1, keepdims=True)
        pb = p.astype(BF16)
        pv = None
        for i, f in enumerate(pvs):
            term = f(pb[:, i * PAGE_SIZE:(i + 1) * PAGE_SIZE])
            pv = term if pv is None else pv + term
        acc_scr[...] = jnp.concatenate([alpha] * (BRANCH_W // LANES), axis=1) * acc_scr[...] + pv
        m_scr[...] = m_new

    q_bd = qbd[...]
    kts = [k_refs[i][...].reshape(BRANCH_W, PAGE_SIZE).astype(BF16) for i in range(P)]
    vts = [v_refs[i][...].reshape(BRANCH_W, PAGE_SIZE).astype(BF16) for i in range(P)]
    page_qk = [_dot(q_bd, kt) for kt in kts]
    page_lf = [lf_refs[i][...] for i in range(P)]
    page_pv = [lambda p, vt=vt: _dot_nt(p, vt) for vt in vts]

    def own_block():
        zpad = jnp.zeros((PAGE_SIZE - tq, BRANCH_W), F32)
        kn = jnp.concatenate([kn_ref[...], zpad], axis=0).astype(BF16)
        vn = jnp.concatenate([vn_ref[...], zpad], axis=0).astype(BF16)
        return _dot_nt(q_bd, kn), lfn_ref[0], lambda p: _dot(p, vn)

    def finish():
        o = acc_scr[...] / jnp.concatenate([l_scr[...]] * (BRANCH_W // LANES), axis=1)
        rows = [jnp.sum(jnp.where(hmask, o[t * FOX_H:(t + 1) * FOX_H], 0.0), axis=0, keepdims=True)
                for t in range(tq)]
        o_ref[...] = jnp.concatenate(rows, axis=0)

    if single:
        qk_n, lf_n, pv_n = own_block()
        update(page_qk + [qk_n], page_lf + [lf_n], True, page_pv + [pv_n])
        finish()
    else:
        update(page_qk, page_lf, False, page_pv)

        @pl.when(j == pl.num_programs(1) - 1)
        def _():
            qk_n, lf_n, pv_n = own_block()
            update([qk_n], [lf_n], True, [pv_n])
            finish()


def _fox_sample(fq, fk32, fv32, lft, layer, k_cache, v_cache, lf_cache, page_table, db, tq):
    n_pages = page_table.shape[1]
    P = next(p for p in (16, 8, 4, 2, 1) if n_pages % p == 0)
    assert tq == SUBLANES
    kt = jnp.transpose(k_cache, (0, 1, 3, 4, 2))
    vt = jnp.transpose(v_cache, (0, 1, 3, 4, 2))
    lfp = jnp.swapaxes(lf_cache, 2, 3)
    lfn = jnp.swapaxes(lft.reshape(FOX_H, db, tq), 0, 1)
    lfn = jnp.pad(lfn, ((0, 0), (0, 0), (0, PAGE_SIZE - tq)))
    tok = lambda: pl.BlockSpec((tq, BRANCH_W), lambda b, j, pt: (b, 0))
    page = lambda i: pl.BlockSpec((None, None, FOX_H, FOX_HD, PAGE_SIZE),
                                  lambda b, j, pt, i=i: (layer, pt[b, j * P + i], 0, 0, 0))
    lfpage = lambda i: pl.BlockSpec((None, None, FOX_H, PAGE_SIZE),
                                    lambda b, j, pt, i=i: (layer, pt[b, j * P + i], 0, 0))
    nrow = tq * FOX_H
    return pl.pallas_call(
        functools.partial(_fox_sample_kernel, P=P, tq=tq, single=n_pages == P),
        grid_spec=pltpu.PrefetchScalarGridSpec(
            num_scalar_prefetch=1, grid=(db, n_pages // P),
            in_specs=[tok(), tok(), tok(), pl.BlockSpec((1, FOX_H, PAGE_SIZE), lambda b, j, pt: (b, 0, 0))]
                     + [page(i) for i in range(P)] + [page(i) for i in range(P)]
                     + [lfpage(i) for i in range(P)],
            out_specs=tok(),
            scratch_shapes=[pltpu.VMEM((nrow, BRANCH_W), BF16), pltpu.VMEM((nrow, LANES), F32),
                            pltpu.VMEM((nrow, LANES), F32), pltpu.VMEM((nrow, BRANCH_W), F32),
                            pltpu.VMEM((FOX_H, LANES), F32)]),
        out_shape=jax.ShapeDtypeStruct((db * tq, BRANCH_W), F32),
        compiler_params=_params('parallel', 'arbitrary'), name='fox_sample')(
            page_table, fq, fk32, fv32, lfn, *([kt] * P), *([vt] * P), *([lfp] * P))


def _merge_kernel(x_ref, og_ref, of_ref, oc_ref, wg_ref, wb_ref, wo_ref, g_ref, b_ref, y_ref, *, precise):
    x = x_ref[...]
    xs = _pieces(x, precise)
    m = None
    for jb, o_ref in enumerate((og_ref, of_ref, oc_ref)):
        gj = _sigmoid(_mm(xs, wg_ref.at[:, :, jb * D_MODEL:(jb + 1) * D_MODEL]))
        term = gj * _mm(_pieces(o_ref[...].astype(F32), precise), wb_ref.at[:, jb])
        m = term if m is None else m + term
    y = DEEPNORM_ALPHA * x + _mm(_pieces(m, precise), wo_ref)
    y_ref[...] = _layernorm(y, g_ref[...], b_ref[...])


def _merge(x, o_gla, o_fox, o_conv, w, precise):
    rows = x.shape[0]
    tm = _row_tile(rows, 512)
    row = lambda n: pl.BlockSpec((tm, n), lambda i: (i, 0))
    consts = [w['wg'], w['wb'], w['wo'], w['ln1_g'], w['ln1_b']]
    return pl.pallas_call(
        functools.partial(_merge_kernel, precise=precise), grid=(rows // tm,),
        in_specs=[row(D_MODEL), row(BRANCH_W), row(BRANCH_W), row(BRANCH_W)]
                 + [_const_spec(a.shape) for a in consts],
        out_specs=row(D_MODEL), out_shape=jax.ShapeDtypeStruct((rows, D_MODEL), F32),
        compiler_params=_params('parallel'), name='merge')(x, o_gla, o_fox, o_conv, *consts)


def _route(probs):
    rows = [probs[i:i + 1, :] for i in range(N_EXPERTS)]
    gscore = []
    for g in range(N_GROUPS):
        r0, r1, r2, r3 = rows[EXP_PER_GROUP * g:EXP_PER_GROUP * (g + 1)]
        a, b = jnp.maximum(r0, r1), jnp.minimum(r0, r1)
        c, d = jnp.maximum(r2, r3), jnp.minimum(r2, r3)
        gscore.append(jnp.maximum(a, c) + jnp.maximum(jnp.minimum(a, c), jnp.maximum(b, d)))
    best, gsel = gscore[0], jnp.zeros_like(gscore[0], dtype=jnp.int32)
    for g in range(1, N_GROUPS):
        upd = gscore[g] > best
        best = jnp.where(upd, gscore[g], best)
        gsel = jnp.where(upd, g, gsel)
    v = []
    for i in range(EXP_PER_GROUP):
        vi = rows[(N_GROUPS - 1) * EXP_PER_GROUP + i]
        for g in range(N_GROUPS - 2, -1, -1):
            vi = jnp.where(gsel == g, rows[g * EXP_PER_GROUP + i], vi)
        v.append(vi)
    v1, i1 = v[0], jnp.zeros_like(gsel)
    for i in range(1, EXP_PER_GROUP):
        upd = v[i] > v1
        v1 = jnp.where(upd, v[i], v1)
        i1 = jnp.where(upd, i, i1)
    v2, i2 = jnp.full_like(v1, -1.0), jnp.full_like(gsel, -1)
    for i in range(EXP_PER_GROUP):
        cand = jnp.where(i1 == i, -1.0, v[i])
        upd = cand > v2
        v2 = jnp.where(upd, cand, v2)
        i2 = jnp.where(upd, i, i2)
    den = v1 + v2
    w1, w2 = v1 / den, v2 / den
    out = []
    for g in range(N_GROUPS):
        for i in range(EXP_PER_GROUP):
            wi = jnp.where(i1 == i, w1, jnp.where(i2 == i, w2, 0.0))
            out.append(jnp.where(gsel == g, wi, 0.0))
    return out


def _moe_kernel(x_ref, wrt_ref, br_ref, wg_ref, wu_ref, wd_ref, g_ref, b_ref, y_ref):
    x = x_ref[...]
    tm = x.shape[0]
    xb = x.astype(BF16)
    x_lo = (x - xb.astype(F32)).astype(BF16)
    wrt = wrt_ref[...]
    w_hi = wrt.astype(BF16)
    w_lo = (wrt - w_hi.astype(F32)).astype(BF16)
    logits = _dot_nt(w_hi, xb) + _dot_nt(w_hi, x_lo) + _dot_nt(w_lo, xb) + br_ref[...]
    e = jnp.exp(logits - jnp.max(logits, axis=0, keepdims=True))
    probs = e / jnp.sum(e, axis=0, keepdims=True)
    gate_rows = _route(probs)
    gate_t = jnp.concatenate(gate_rows + [jnp.zeros((LANES - N_EXPERTS, tm), F32)], axis=0)
    gate = gate_t.T
    acc = jnp.zeros((tm, D_MODEL), F32)
    for ex in range(N_EXPERTS):
        hg = _dot(xb, wg_ref[ex])
        hu = _dot(xb, wu_ref[ex])
        h = (hg * _sigmoid(hg)) * hu * gate[:, ex:ex + 1]
        acc = acc + _dot(h.astype(BF16), wd_ref[ex])
    y = DEEPNORM_ALPHA * x + acc
    y_ref[...] = _layernorm(y, g_ref[...], b_ref[...])


def _moe(x, w):
    rows = x.shape[0]
    tm = _row_tile(rows, 512)
    row = pl.BlockSpec((tm, D_MODEL), lambda i: (i, 0))
    consts = [w['wrt'], w['br'], w['weg'], w['weu'], w['wed'], w['ln2_g'], w['ln2_b']]
    return pl.pallas_call(
        _moe_kernel, grid=(rows // tm,),
        in_specs=[row] + [_const_spec(a.shape) for a in consts],
        out_specs=row, out_shape=jax.ShapeDtypeStruct((rows, D_MODEL), F32),
        compiler_params=_params('parallel'), name='moe')(x, *consts)


def _prep_weights(l, precise, w_in, w_gla_a2, b_gla_a, gla_norm_g, b_fox_f, conv_w, w_branch, w_out, ln1_g, ln1_b,
                  w_router, b_router, w_e_gate, w_e_up, w_e_down, ln2_g, ln2_b):
    wi = w_in[l]
    off = {}
    o = 0
    for name, n in (('gla_q', 256), ('gla_k', 256), ('gla_v', 512), ('gla_a', GLA_LR), ('gla_r', 512),
                    ('fox_q', 512), ('fox_k', 512), ('fox_v', 512), ('fox_f', FOX_H),
                    ('conv', 3 * CONV_D), ('gate', N_BRANCH * D_MODEL)):
        off[name] = (o, o + n)
        o += n
    col = lambda name: wi[:, off[name][0]:off[name][1]]
    padc = lambda a, n: jnp.pad(a, ((0, 0), (0, n - a.shape[1])))
    row2 = lambda a: a.reshape(1, -1)

    def bf(a):
        hi = a.astype(BF16)
        return jnp.stack([hi, (a - hi.astype(F32)).astype(BF16)]) if precise else hi[None]

    ff = col('fox_f')
    return {
        'wqkv': bf(jnp.concatenate([col('gla_q') * (GLA_DK ** -0.5), col('gla_k'), col('gla_v')], axis=1)),
        'wa': bf(padc(col('gla_a'), LANES)),
        'wa2': bf(jnp.pad(w_gla_a2[l], ((0, LANES - GLA_LR), (0, 0)))),
        'ba': row2(b_gla_a[l]),
        'wr': bf(col('gla_r')),
        'wfq': bf((col('fox_q') * (FOX_HD ** -0.5))),
        'wfk': bf(col('fox_k')),
        'wfv': bf(col('fox_v')),
        'wfkt': bf(col('fox_k').T),
        'wfvt': bf(col('fox_v').T),
        'wff': bf(padc(ff, LANES)),
        'bff': padc(row2(b_fox_f[l]), LANES),
        'wfft': bf(jnp.pad(ff.T, ((0, 2 * SUBLANES - FOX_H), (0, 0)))),
        'bfft': b_fox_f[l].reshape(FOX_H, 1),
        'pad_ones': (jnp.arange(BRANCH_W)[None, :] % FOX_HD < jnp.arange(1, SUBLANES + 1)[:, None]).astype(F32),
        'wc': bf(col('conv')),
        'wg': bf(col('gate')),
        'gnorm': row2(gla_norm_g[l]),
        'conv_w': conv_w[l],
        'wb': bf(w_branch[l]),
        'wo': bf(w_out[l]),
        'ln1_g': row2(ln1_g[l]), 'ln1_b': row2(ln1_b[l]),
        'wrt': w_router.T, 'br': b_router.reshape(N_EXPERTS, 1),
        'weg': w_e_gate[l].astype(BF16), 'weu': w_e_up[l].astype(BF16), 'wed': w_e_down[l].astype(BF16),
        'ln2_g': row2(ln2_g[l]), 'ln2_b': row2(ln2_b[l]),
    }


TAIL = 256
SPLICE = 128


def _layer(x, w, bg, tg, gla_s0, conv_buf, pools, act_dtype, precise):
    prompt = pools is None
    precise = precise and prompt and tg >= 2 * TAIL
    gqkv, gla, gr, fk, fv, lf, *fox_in = _proj1(x, w, prompt, tg, precise)
    o_gla, gla_state = _gla(gqkv, gla, gr, gla_s0, w['gnorm'], bg, tg, F32 if precise else act_dtype, precise)
    if prompt:
        o_fox = _fox_prompt(fox_in[:3], bg, tg)
        caches = (fk.reshape(bg, FOX_H, FOX_HD, tg), fv.reshape(bg, FOX_H, FOX_HD, tg), lf)
    else:
        o_fox = _fox_sample(fox_in[0], fk, fv, fox_in[1], *pools, bg, tg)
        caches = (fk.reshape(bg, tg, FOX_H, FOX_HD), fv.reshape(bg, tg, FOX_H, FOX_HD), lf.reshape(bg, tg, FOX_H))
    o_conv, conv_state = _conv(x, w['wc'], conv_buf, w['conv_w'], bg, tg, act_dtype, False)
    y = _moe(_merge(x, o_gla, o_fox, o_conv, w, False), w)
    if precise:
        n_t = TAIL // min(tg, 256)
        tail = lambda a: a.reshape(bg, tg, a.shape[-1])[:, tg - TAIL:].reshape(bg * TAIL, a.shape[-1])
        xt = tail(x)
        o_fox_t = _fox_prompt(fox_in, bg, tg, n_run=n_t)
        o_conv_t, _ = _conv(xt, w['wc'], jnp.zeros_like(conv_buf), w['conv_w'], bg, TAIL, F32, True)
        y_t = _moe(_merge(xt, tail(o_gla), o_fox_t, o_conv_t, w, True), w)
        y = y.reshape(bg, tg, D_MODEL).at[:, tg - SPLICE:].set(
            y_t.reshape(bg, TAIL, D_MODEL)[:, TAIL - SPLICE:]).reshape(bg * tg, D_MODEL)
    return y, caches + (gla_state, conv_state)


def kernel(x_prompt, x_sample, cache_fox_k, cache_fox_v, cache_fox_logf, state_gla, state_conv, page_table, w_in, w_gla_a2, b_gla_a, gla_norm_g, b_fox_f, conv_w, w_branch, w_out, ln1_g, ln1_b, w_router, b_router, w_e_gate, w_e_up, w_e_down, ln2_g, ln2_b):
    B, T, _ = x_prompt.shape
    DB, TS, _ = x_sample.shape
    xp = x_prompt.reshape(B * T, D_MODEL)
    xs = x_sample.reshape(DB * TS, D_MODEL)
    depth = w_in.shape[0]
    st_p, st_s = [], []
    for l in range(depth):
        precise = l < depth - 1
        w = _prep_weights(l, precise, w_in, w_gla_a2, b_gla_a, gla_norm_g, b_fox_f, conv_w, w_branch, w_out,
                          ln1_g, ln1_b, w_router, b_router, w_e_gate, w_e_up, w_e_down, ln2_g, ln2_b)
        xp, sp = _layer(xp, w, B, T, jnp.zeros((B, GLA_H, GLA_DK, GLA_DV), F32),
                        jnp.zeros((B, CONV_W - 1, CONV_D), F32), None, BF16, precise)
        xs, ss = _layer(xs, w, DB, TS, state_gla[l], state_conv[l],
                        (l, cache_fox_k, cache_fox_v, cache_fox_logf, page_table), F32, False)
        st_p.append(sp)
        st_s.append(ss)
    stack = lambda sts, i: jnp.stack([s[i] for s in sts])
    return (xp.reshape(B, T, D_MODEL), xs.reshape(DB, TS, D_MODEL),
            jnp.transpose(stack(st_p, 0), (0, 1, 4, 2, 3)), jnp.transpose(stack(st_p, 1), (0, 1, 4, 2, 3)),
            jnp.swapaxes(stack(st_p, 2), 2, 3), stack(st_p, 3), stack(st_p, 4),
            stack(st_s, 0), stack(st_s, 1), stack(st_s, 2), stack(st_s, 3), stack(st_s, 4))
```
